```python
import math
import jax, jax.numpy as jnp
from jax import lax
import numpy as np

D_MODEL = 2048
BATCH = 4
SEQ = 4096
DEPTH = 4

N_MIXERS = 3
FOX_HEADS = 16
FOX_HEAD_DIM = D_MODEL // FOX_HEADS
FOX_Q_BLOCK = 128
GLA_HEADS = 4
GLA_KEY_DIM = D_MODEL // 2
GLA_VALUE_DIM = D_MODEL
GLA_DK = GLA_KEY_DIM // GLA_HEADS
GLA_DV = GLA_VALUE_DIM // GLA_HEADS
GLA_GATE_RANK = 16
GLA_GATE_TAU = 16.0
GLA_CHUNK = 64
CONV_WIDTH = 3
D_FF = 5632
LN_EPS = 1e-5
RMS_EPS = 1e-5
DEEPNORM_ALPHA = (2 * DEPTH) ** 0.25
DEEPNORM_BETA = (8 * DEPTH) ** -0.25
N_FOX = (DEPTH + 2) // 3
N_GLA = (DEPTH + 1) // 3
N_CONV = DEPTH // 3

kernel_name = 'hybrid_fox_gla_shortconv_deepnorm_adaln'


def layer_norm(x, g, b):
    xf = x.astype(jnp.float32)
    mu = jnp.mean(xf, axis=-1, keepdims=True)
    var = jnp.mean(jnp.square(xf - mu), axis=-1, keepdims=True)
    return ((xf - mu) * lax.rsqrt(var + LN_EPS) * g + b).astype(x.dtype)


def causal_dwconv(x, w):
    s = x.shape[1]
    xp = jnp.pad(x, ((0, 0), (CONV_WIDTH - 1, 0), (0, 0)))
    y = xp[:, 0:s] * w[0]
    for k in range(1, CONV_WIDTH):
        y = y + xp[:, k:k + s] * w[k]
    return y


def fox_mixer(h, wq, wk, wv, wg, wf, bf, wo):
    b, s, _ = h.shape
    nq = s // FOX_Q_BLOCK

    def heads(w):
        return (h @ w).reshape(b, s, FOX_HEADS, FOX_HEAD_DIM).transpose(0, 2, 1, 3)

    q = heads(wq) * (FOX_HEAD_DIM ** -0.5)
    k = heads(wk)
    v = heads(wv)
    log_f = jax.nn.log_sigmoid((h @ wf + bf).astype(jnp.float32))
    cum = jnp.cumsum(log_f, axis=1).transpose(0, 2, 1)
    pos = jnp.arange(s)
    q_blocks = q.reshape(b, FOX_HEADS, nq, FOX_Q_BLOCK, FOX_HEAD_DIM).transpose(2, 0, 1, 3, 4)
    f_blocks = cum.reshape(b, FOX_HEADS, nq, FOX_Q_BLOCK).transpose(2, 0, 1, 3)
    p_blocks = pos.reshape(nq, FOX_Q_BLOCK)

    def attend(args):
        q_blk, f_blk, p_blk = args
        logits = jnp.einsum('bhqd,bhkd->bhqk', q_blk, k).astype(jnp.float32)
        logits = logits + f_blk[..., None] - cum[:, :, None, :]
        logits = jnp.where(pos[None, :] <= p_blk[:, None], logits, -jnp.inf)
        probs = jax.nn.softmax(logits, axis=-1)
        return jnp.einsum('bhqk,bhkd->bhqd', probs.astype(v.dtype), v)

    o = lax.map(attend, (q_blocks, f_blocks, p_blocks))
    o = o.transpose(1, 0, 3, 2, 4).reshape(b, s, FOX_HEADS * FOX_HEAD_DIM)
    o = o * jax.nn.sigmoid(h @ wg)
    return o @ wo


def gla_mixer(h, wq, wk, wv, wa1, wa2, ba, wr, norm_g, wo):
    b, s, _ = h.shape
    nc = s // GLA_CHUNK
    f32 = jnp.float32

    def heads(t, d):
        return t.astype(f32).reshape(b, nc, GLA_CHUNK, GLA_HEADS, d).transpose(0, 3, 1, 2, 4)

    q = heads(h @ wq, GLA_DK) * (GLA_DK ** -0.5)
    k = heads(h @ wk, GLA_DK)
    v = heads(h @ wv, GLA_DV)
    log_a = jax.nn.log_sigmoid(((h @ wa1) @ wa2 + ba).astype(f32)) / GLA_GATE_TAU
    log_a = heads(log_a, GLA_DK)
    cb = jnp.cumsum(log_a, axis=3)
    cb_last = cb[:, :, :, -1]
    q_dec = q * jnp.exp(cb)
    k_inv = k * jnp.exp(-cb)
    k_dec = k * jnp.exp(cb_last[:, :, :, None, :] - cb)
    causal = jnp.tril(jnp.ones((GLA_CHUNK, GLA_CHUNK), dtype=bool))
    att = jnp.einsum('bhnid,bhnjd->bhnij', q_dec, k_inv)
    att = jnp.where(causal, att, 0.0)
    o_intra = jnp.einsum('bhnij,bhnjv->bhniv', att, v)

    def step(state, xs):
        qd, kd, vc, bl = xs
        o_c = jnp.einsum('bhcd,bhdv->bhcv', qd, state)
        state = jnp.exp(bl)[..., None] * state + jnp.einsum('bhcd,bhcv->bhdv', kd, vc)
        return state, o_c

    def chunk_major(t):
        return jnp.moveaxis(t, 2, 0)

    s0 = jnp.zeros((b, GLA_HEADS, GLA_DK, GLA_DV), f32)
    _, o_inter = lax.scan(step, s0, (chunk_major(q_dec), chunk_major(k_dec),
                                     chunk_major(v), chunk_major(cb_last)))
    o = o_intra + jnp.moveaxis(o_inter, 0, 2)
    o = o * lax.rsqrt(jnp.mean(o * o, axis=-1, keepdims=True) + RMS_EPS) * norm_g
    o = o.transpose(0, 2, 3, 1, 4).reshape(b, s, GLA_VALUE_DIM).astype(h.dtype)
    o = o * jax.nn.silu(h @ wr)
    return o @ wo


def short_conv_mixer(h, w_in, conv_w, w_out):
    gate_b, gate_c, u = jnp.split(h @ w_in, 3, axis=-1)
    return (gate_b * causal_dwconv(gate_c * u, conv_w)) @ w_out


def conv_ffn(h, w_up, conv_w, conv_b, w_down):
    z = causal_dwconv(h @ w_up, conv_w) + conv_b
    a, u = jnp.split(z, 2, axis=-1)
    return (jax.nn.silu(a) * u) @ w_down


def setup_inputs(seed: int = 0) -> dict:
    key = jax.random.key(seed)
    keys = list(jax.random.split(key, 40))
    counter = [0]

    def nrm(shape, scale):
        kk = keys[counter[0]]
        counter[0] += 1
        return jax.random.normal(kk, shape, jnp.float32) * scale

    D = D_MODEL
    beta = DEEPNORM_BETA
    x = nrm((BATCH, SEQ, D), 1.0)
    c = nrm((BATCH, D), 1.0)
    ada_w = nrm((DEPTH, D, 6 * D), D ** -0.5)
    ada_b = nrm((DEPTH, 6 * D), 0.02)
    ln1_g = 1.0 + nrm((DEPTH, D), 0.02)
    ln1_b = nrm((DEPTH, D), 0.02)
    ln2_g = 1.0 + nrm((DEPTH, D), 0.02)
    ln2_b = nrm((DEPTH, D), 0.02)
    fox_wq = nrm((N_FOX, D, D), D ** -0.5)
    fox_wk = nrm((N_FOX, D, D), D ** -0.5)
    fox_wv = nrm((N_FOX, D, D), D ** -0.5)
    fox_wg = nrm((N_FOX, D, D), D ** -0.5)
    fox_wf = nrm((N_FOX, D, FOX_HEADS), D ** -0.5)
    fox_bf = 2.0 + nrm((N_FOX, FOX_HEADS), 0.5)
    fox_wo = nrm((N_FOX, D, D), D ** -0.5 * beta)
    gla_wq = nrm((N_GLA, D, GLA_KEY_DIM), D ** -0.5)
    gla_wk = nrm((N_GLA, D, GLA_KEY_DIM), D ** -0.5)
    gla_wv = nrm((N_GLA, D, GLA_VALUE_DIM), D ** -0.5)
    gla_wa1 = nrm((N_GLA, D, GLA_GATE_RANK), D ** -0.5)
    gla_wa2 = nrm((N_GLA, GLA_GATE_RANK, GLA_KEY_DIM), GLA_GATE_RANK ** -0.5)
    gla_ba = nrm((N_GLA, GLA_KEY_DIM), 0.1)
    gla_wr = nrm((N_GLA, D, GLA_VALUE_DIM), D ** -0.5)
    gla_norm_g = 1.0 + nrm((N_GLA, GLA_DV), 0.02)
    gla_wo = nrm((N_GLA, GLA_VALUE_DIM, D), GLA_VALUE_DIM ** -0.5 * beta)
    conv_w_in = nrm((N_CONV, D, 3 * D), D ** -0.5)
    conv_w = nrm((N_CONV, CONV_WIDTH, D), CONV_WIDTH ** -0.5)
    conv_w_out = nrm((N_CONV, D, D), D ** -0.5 * beta)
    ffn_w_up = nrm((DEPTH, D, 2 * D_FF), D ** -0.5)
    ffn_conv_w = nrm((DEPTH, CONV_WIDTH, 2 * D_FF), CONV_WIDTH ** -0.5)
    ffn_conv_b = nrm((DEPTH, 2 * D_FF), 0.02)
    ffn_w_down = nrm((DEPTH, D_FF, D), D_FF ** -0.5 * beta)
    return {'x': x, 'c': c, 'ada_w': ada_w, 'ada_b': ada_b,
            'ln1_g': ln1_g, 'ln1_b': ln1_b, 'ln2_g': ln2_g, 'ln2_b': ln2_b,
            'fox_wq': fox_wq, 'fox_wk': fox_wk, 'fox_wv': fox_wv, 'fox_wg': fox_wg,
            'fox_wf': fox_wf, 'fox_bf': fox_bf, 'fox_wo': fox_wo,
            'gla_wq': gla_wq, 'gla_wk': gla_wk, 'gla_wv': gla_wv, 'gla_wa1': gla_wa1,
            'gla_wa2': gla_wa2, 'gla_ba': gla_ba, 'gla_wr': gla_wr,
            'gla_norm_g': gla_norm_g, 'gla_wo': gla_wo,
            'conv_w_in': conv_w_in, 'conv_w': conv_w, 'conv_w_out': conv_w_out,
            'ffn_w_up': ffn_w_up, 'ffn_conv_w': ffn_conv_w, 'ffn_conv_b': ffn_conv_b,
            'ffn_w_down': ffn_w_down}


def reference(x, c, ada_w, ada_b, ln1_g, ln1_b, ln2_g, ln2_b,
              fox_wq, fox_wk, fox_wv, fox_wg, fox_wf, fox_bf, fox_wo,
              gla_wq, gla_wk, gla_wv, gla_wa1, gla_wa2, gla_ba, gla_wr,
              gla_norm_g, gla_wo,
              conv_w_in, conv_w, conv_w_out,
              ffn_w_up, ffn_conv_w, ffn_conv_b, ffn_w_down):
    cond = jax.nn.silu(c)
    for i in range(DEPTH):
        mod = cond @ ada_w[i] + ada_b[i]
        sh1, sc1, g1, sh2, sc2, g2 = [m[:, None, :] for m in jnp.split(mod, 6, axis=-1)]
        h = x * (1.0 + sc1) + sh1
        kind = i % N_MIXERS
        j = i // N_MIXERS
        if kind == 0:
            y = fox_mixer(h, fox_wq[j], fox_wk[j], fox_wv[j], fox_wg[j],
                          fox_wf[j], fox_bf[j], fox_wo[j])
        elif kind == 1:
            y = gla_mixer(h, gla_wq[j], gla_wk[j], gla_wv[j], gla_wa1[j], gla_wa2[j],
                          gla_ba[j], gla_wr[j], gla_norm_g[j], gla_wo[j])
        else:
            y = short_conv_mixer(h, conv_w_in[j], conv_w[j], conv_w_out[j])
        x = layer_norm(DEEPNORM_ALPHA * x + g1 * y, ln1_g[i], ln1_b[i])
        h = x * (1.0 + sc2) + sh2
        y = conv_ffn(h, ffn_w_up[i], ffn_conv_w[i], ffn_conv_b[i], ffn_w_down[i])
        x = layer_norm(DEEPNORM_ALPHA * x + g2 * y, ln2_g[i], ln2_b[i])
    return x
```

```python
import functools

import numpy as np
import jax
import jax.numpy as jnp
from jax import lax
from jax.experimental import pallas as pl
from jax.experimental.pallas import tpu as pltpu

F32 = jnp.float32
BF16 = jnp.bfloat16

LANES = 128
SUBLANES = 8
MOD_ROWS = SUBLANES
MIB = 1 << 20

GLA_CHUNK = 64
GLA_GATE_TAU = 16.0
LN_EPS = 1e-5
RMS_EPS = 1e-5

TILE = dict(
    ada_n=1024,
    rows=512,
    proj_m=1024,
    proj_n=512,
    attn=512,
    gla_rows=4 * GLA_CHUNK,
    ln_m_resident=256,
    ln_m_ktiled=512,
    ln_k=512,
    ln_k_resident_max=2048,
)

_NT = (((1,), (1,)), ((), ()))
_TN = (((0,), (0,)), ((), ()))


def _params(semantics, vmem_mib):
    return pltpu.CompilerParams(dimension_semantics=semantics,
                                vmem_limit_bytes=vmem_mib * MIB)


def _tile(n, pref):
    t = min(n, pref)
    assert n % t == 0, (n, pref)
    return t


def _dot(a, b):
    return jnp.dot(a, b, preferred_element_type=F32)


def _log_sigmoid(z):
    return jnp.minimum(z, 0.0) - jnp.log1p(jnp.exp(-jnp.abs(z)))


def _silu(z):
    return z * jax.nn.sigmoid(z)


def _split3(x):
    hi = x.astype(BF16)
    r1 = x - hi.astype(F32)
    mid = r1.astype(BF16)
    lo = (r1 - mid.astype(F32)).astype(BF16)
    return hi, mid, lo


def _tril_mask(n):
    r = lax.broadcasted_iota(jnp.int32, (n, n), 0)
    c = lax.broadcasted_iota(jnp.int32, (n, n), 1)
    return c <= r


def _cumsum_rows(tril_bf16, x):
    hi, mid, lo = _split3(x)
    return (_dot(tril_bf16, lo) + _dot(tril_bf16, mid)) + _dot(tril_bf16, hi)


def _causal_conv3(x, prev, w_ref):
    rows = lax.broadcasted_iota(jnp.int32, x.shape, 0)
    p6 = prev[SUBLANES - 2:SUBLANES - 1, :]
    p7 = prev[SUBLANES - 1:SUBLANES, :]
    x1 = jnp.where(rows == 0, p7, pltpu.roll(x, 1, 0))
    x2 = jnp.where(rows == 0, p6, jnp.where(rows == 1, p7, pltpu.roll(x, 2, 0)))
    return x2 * w_ref[0:1, :] + x1 * w_ref[1:2, :] + x * w_ref[2:3, :]


def _ada_kernel(c_ref, w_ref, b_ref, o_ref):
    cond = _silu(c_ref[...]).astype(BF16)
    o_ref[0] = _dot(cond, w_ref[0].astype(BF16)) + b_ref[0]


def _ada_mod(c_pad, ada_w, ada_b):
    depth, d, n = ada_w.shape
    tn = _tile(n, TILE["ada_n"])
    return pl.pallas_call(
        _ada_kernel,
        grid=(depth, n // tn),
        in_specs=[
            pl.BlockSpec((MOD_ROWS, d), lambda i, j: (0, 0)),
            pl.BlockSpec((1, d, tn), lambda i, j: (i, 0, j)),
            pl.BlockSpec((1, 1, tn), lambda i, j: (i, 0, j)),
        ],
        out_specs=pl.BlockSpec((1, MOD_ROWS, tn), lambda i, j: (i, 0, j)),
        out_shape=jax.ShapeDtypeStruct((depth, MOD_ROWS, n), F32),
        compiler_params=_params(("arbitrary", "arbitrary"), 40),
        name="ada_mod",
    )(c_pad, ada_w, ada_b.reshape(depth, 1, n))


def _mod_spec(d, layer, chunk, rows_per_batch):
    def index(m, *_):
        return ((layer * MOD_ROWS + m // rows_per_batch) * 6 + chunk, 0, 0)
    return pl.BlockSpec((1, 1, d), index)


def _modulate_kernel(x_ref, sc_ref, sh_ref, h_ref):
    h_ref[...] = (x_ref[...] * (1.0 + sc_ref[0]) + sh_ref[0]).astype(BF16)


def _modulate(x2, mod_tab, layer, seq):
    t, d = x2.shape
    tm = _tile(seq, TILE["rows"])
    return pl.pallas_call(
        _modulate_kernel,
        grid=(t // tm,),
        in_specs=[
            pl.BlockSpec((tm, d), lambda m: (m, 0)),
            _mod_spec(d, layer, 1, seq // tm),
            _mod_spec(d, layer, 0, seq // tm),
        ],
        out_specs=pl.BlockSpec((tm, d), lambda m: (m, 0)),
        out_shape=jax.ShapeDtypeStruct((t, d), BF16),
        compiler_params=_params(("arbitrary",), 32),
        name="modulate",
    )(x2, mod_tab, mod_tab)


def _multi_proj_kernel(h_ref, *refs, posts):
    n = len(posts)
    h = h_ref[...]
    for w_ref, o_ref, post in zip(refs[:n], refs[n:], posts):
        o_ref[...] = post(_dot(h, w_ref[...])).astype(o_ref.dtype)


def _multi_proj(h, weights, posts, out_dtypes, name):
    t, d = h.shape
    n = weights[0].shape[1]
    tm, tn = _tile(t, TILE["proj_m"]), _tile(n, TILE["proj_n"])
    w_spec = pl.BlockSpec((d, tn), lambda j, m: (0, j))
    o_spec = pl.BlockSpec((tm, tn), lambda j, m: (m, j))
    return pl.pallas_call(
        functools.partial(_multi_proj_kernel, posts=tuple(posts)),
        grid=(n // tn, t // tm),
        in_specs=[pl.BlockSpec((tm, d), lambda j, m: (m, 0))] + [w_spec] * len(weights),
        out_specs=[o_spec] * len(weights),
        out_shape=[jax.ShapeDtypeStruct((t, n), dt) for dt in out_dtypes],
        compiler_params=_params(("arbitrary", "arbitrary"), 48),
        name=name,
    )(h, *weights)


def _fox_cum_kernel(h_ref, wf_ref, bf_ref, col_ref, row_ref, carry_ref):
    @pl.when(pl.program_id(1) == 0)
    def _():
        carry_ref[...] = jnp.zeros_like(carry_ref)

    log_f = _log_sigmoid(_dot(h_ref[...], wf_ref[...]) + bf_ref[...])
    ts = log_f.shape[0]
    tril = jnp.where(_tril_mask(ts), 1.0, 0.0).astype(BF16)
    cum = _cumsum_rows(tril, log_f) + carry_ref[...]
    col_ref[0] = cum
    row_ref[0] = cum.T
    carry_ref[...] = cum[ts - 1:ts, :]


def _fox_cum(h, wf_pad, bf_pad, batch, seq):
    t, d = h.shape
    ts = _tile(seq, TILE["rows"])
    ns = seq // ts
    return pl.pallas_call(
        _fox_cum_kernel,
        grid=(batch, ns),
        in_specs=[
            pl.BlockSpec((ts, d), lambda b, s: (b * ns + s, 0)),
            pl.BlockSpec((d, LANES), lambda b, s: (0, 0)),
            pl.BlockSpec((1, LANES), lambda b, s: (0, 0)),
        ],
        out_specs=[
            pl.BlockSpec((1, ts, LANES), lambda b, s: (b, s, 0)),
            pl.BlockSpec((1, LANES, ts), lambda b, s: (b, 0, s)),
        ],
        out_shape=[
            jax.ShapeDtypeStruct((batch, seq, LANES), F32),
            jax.ShapeDtypeStruct((batch, LANES, seq), F32),
        ],
        scratch_shapes=[pltpu.VMEM((1, LANES), F32)],
        compiler_params=_params(("arbitrary", "arbitrary"), 32),
        name="fox_cum",
    )(h, wf_pad, bf_pad)


def _fox_attn_kernel(q_ref, k_ref, v_ref, fcol_ref, frow_ref, g_ref, o_ref,
                     m_ref, l_ref, acc_ref, *, tq):
    head = pl.program_id(1)
    qi = pl.program_id(2)
    q = q_ref[...]
    lane = lax.broadcasted_iota(jnp.int32, fcol_ref.shape, 1)
    f_t = jnp.sum(jnp.where(lane == head, fcol_ref[...], 0.0), axis=-1, keepdims=True)
    m_ref[...] = jnp.full(m_ref.shape, -jnp.inf, F32)
    l_ref[...] = jnp.zeros(l_ref.shape, F32)
    acc_ref[...] = jnp.zeros(acc_ref.shape, F32)

    def block(j, diagonal):
        start = pl.multiple_of(j * tq, tq)
        k = k_ref[pl.ds(start, tq), :]
        v = v_ref[pl.ds(start, tq), :]
        f_s = frow_ref[pl.ds(j, 1), :]
        s = lax.dot_general(q, k, _NT, preferred_element_type=F32) + (f_t - f_s)
        if diagonal:
            s = jnp.where(_tril_mask(tq), s, -jnp.inf)
        m_prev = m_ref[...]
        m_new = jnp.maximum(m_prev, jnp.max(s, axis=-1, keepdims=True))
        alpha = jnp.exp(m_prev - m_new)
        p = jnp.exp(s - m_new)
        l_ref[...] = alpha * l_ref[...] + jnp.sum(p, axis=-1, keepdims=True)
        acc_ref[...] = alpha * acc_ref[...] + _dot(p.astype(BF16), v)
        m_ref[...] = m_new

    def body(j, carry):
        block(j, False)
        return carry

    lax.fori_loop(0, qi, body, 0)
    block(qi, True)
    o_ref[...] = ((acc_ref[...] / l_ref[...]) * g_ref[...]).astype(o_ref.dtype)


def _fox_attn(q, k, v, g, fcol, frow, heads):
    batch, seq, d = q.shape
    dh = d // heads
    assert dh == LANES and heads <= LANES
    tq = _tile(seq, TILE["attn"])
    nq = seq // tq
    frow4 = frow.reshape(batch, LANES, nq, tq)
    qspec = pl.BlockSpec((None, tq, dh), lambda b, h, i: (b, i, h))
    kvspec = pl.BlockSpec((None, seq, dh), lambda b, h, i: (b, 0, h))
    return pl.pallas_call(
        functools.partial(_fox_attn_kernel, tq=tq),
        grid=(batch, heads, nq),
        in_specs=[
            qspec, kvspec, kvspec,
            pl.BlockSpec((None, tq, LANES), lambda b, h, i: (b, i, 0)),
            pl.BlockSpec((None, None, nq, tq), lambda b, h, i: (b, h, 0, 0)),
            qspec,
        ],
        out_specs=qspec,
        out_shape=jax.ShapeDtypeStruct((batch, seq, d), BF16),
        scratch_shapes=[
            pltpu.VMEM((tq, 1), F32),
            pltpu.VMEM((tq, 1), F32),
            pltpu.VMEM((tq, dh), F32),
        ],
        compiler_params=_params(("arbitrary", "arbitrary", "arbitrary"), 32),
        name="fox_attn",
    )(q, k, v, fcol, frow4, g)


def _gla_kernel(q_ref, k_ref, v_ref, a1_ref, wa2_ref, ba_ref, ng_ref, r_ref, o_ref,
                state_ref, *, n_sub):
    @pl.when(pl.program_id(2) == 0)
    def _():
        state_ref[...] = jnp.zeros_like(state_ref)

    c = GLA_CHUNK
    causal = _tril_mask(c)
    tril = jnp.where(causal, 1.0, 0.0).astype(BF16)
    ones = jnp.ones((c, LANES), BF16)
    for sub in range(n_sub):
        rows = slice(sub * c, (sub + 1) * c)
        gate = _dot(a1_ref[rows, :].astype(BF16), wa2_ref[...]) + ba_ref[...]
        log_a = _log_sigmoid(gate) / GLA_GATE_TAU
        la_hi, la_mid, la_lo = _split3(log_a)
        cb = (_dot(tril, la_lo) + _dot(tril, la_mid)) + _dot(tril, la_hi)
        cb_last = cb[c - 1:c, :]
        decay_col = ((lax.dot_general(la_lo, ones, _TN, preferred_element_type=F32)
                      + lax.dot_general(la_mid, ones, _TN, preferred_element_type=F32))
                     + lax.dot_general(la_hi, ones, _TN, preferred_element_type=F32))
        q = q_ref[rows, :]
        k = k_ref[rows, :]
        v = v_ref[rows, :]
        q_dec = (q * jnp.exp(cb)).astype(BF16)
        k_inv = (k * jnp.exp(-cb)).astype(BF16)
        k_dec = (k * jnp.exp(cb_last - cb)).astype(BF16)
        att = lax.dot_general(q_dec, k_inv, _NT, preferred_element_type=F32)
        att = jnp.where(causal, att, 0.0)
        state = state_ref[...]
        o = _dot(att.astype(BF16), v) + _dot(q_dec, state.astype(BF16))
        state_ref[...] = (jnp.exp(decay_col[:, 0:1]) * state
                          + lax.dot_general(k_dec, v, _TN, preferred_element_type=F32))
        o = o * lax.rsqrt(jnp.mean(o * o, axis=-1, keepdims=True) + RMS_EPS) * ng_ref[...]
        o_ref[rows, :] = (o * r_ref[rows, :]).astype(o_ref.dtype)


def _gla(q, k, v, a1, wa2_pad, ba, norm_g, r, batch, seq):
    t, kdim = q.shape
    vdim = v.shape[1]
    dv = norm_g.shape[-1]
    heads = vdim // dv
    dk = kdim // heads
    rows = _tile(seq, TILE["gla_rows"])
    ns = seq // rows

    def row_spec(width):
        return pl.BlockSpec((rows, width), lambda b, h, s: (b * ns + s, h))

    return pl.pallas_call(
        functools.partial(_gla_kernel, n_sub=rows // GLA_CHUNK),
        grid=(batch, heads, ns),
        in_specs=[
            row_spec(dk), row_spec(dk), row_spec(dv),
            pl.BlockSpec((rows, LANES), lambda b, h, s: (b * ns + s, 0)),
            pl.BlockSpec((LANES, dk), lambda b, h, s: (0, h)),
            pl.BlockSpec((1, dk), lambda b, h, s: (0, h)),
            pl.BlockSpec((1, dv), lambda b, h, s: (0, 0)),
            row_spec(dv),
        ],
        out_specs=row_spec(dv),
        out_shape=jax.ShapeDtypeStruct((t, vdim), BF16),
        scratch_shapes=[pltpu.VMEM((dk, dv), F32)],
        compiler_params=_params(("arbitrary", "arbitrary", "arbitrary"), 32),
        name="gla_chunk",
    )(q, k, v, a1, wa2_pad, ba, norm_g, r)


def _conv_mix_kernel(h_ref, wb_ref, wc_ref, wu_ref, cw_ref, o_ref, carry_ref, *, tiles_per_seq):
    @pl.when(pl.program_id(1) % tiles_per_seq == 0)
    def _():
        carry_ref[...] = jnp.zeros_like(carry_ref)

    h = h_ref[...]
    cu = _dot(h, wc_ref[...]) * _dot(h, wu_ref[...])
    y = _causal_conv3(cu, carry_ref[...], cw_ref)
    carry_ref[...] = cu[cu.shape[0] - SUBLANES:, :]
    o_ref[...] = (_dot(h, wb_ref[...]) * y).astype(o_ref.dtype)


def _conv_mix(h, w_in, conv_w, seq):
    t, d = h.shape
    n = w_in.shape[1] // 3
    tm, tn = _tile(seq, TILE["proj_m"]), _tile(n, TILE["proj_n"])
    nj = n // tn
    return pl.pallas_call(
        functools.partial(_conv_mix_kernel, tiles_per_seq=seq // tm),
        grid=(nj, t // tm),
        in_specs=[
            pl.BlockSpec((tm, d), lambda j, m: (m, 0)),
            pl.BlockSpec((d, tn), lambda j, m: (0, j)),
            pl.BlockSpec((d, tn), lambda j, m: (0, nj + j)),
            pl.BlockSpec((d, tn), lambda j, m: (0, 2 * nj + j)),
            pl.BlockSpec((conv_w.shape[0], tn), lambda j, m: (0, j)),
        ],
        out_specs=pl.BlockSpec((tm, tn), lambda j, m: (m, j)),
        out_shape=jax.ShapeDtypeStruct((t, n), BF16),
        scratch_shapes=[pltpu.VMEM((SUBLANES, tn), F32)],
        compiler_params=_params(("arbitrary", "arbitrary"), 48),
        name="conv_mix",
    )(h, w_in, w_in, w_in, conv_w)


def _ffn_up_kernel(h_ref, wa_ref, wu_ref, cwa_ref, cwu_ref, ba_ref, bu_ref, o_ref,
                   carry_a_ref, carry_u_ref, *, tiles_per_seq):
    @pl.when(pl.program_id(1) % tiles_per_seq == 0)
    def _():
        carry_a_ref[...] = jnp.zeros_like(carry_a_ref)
        carry_u_ref[...] = jnp.zeros_like(carry_u_ref)

    h = h_ref[...]
    za = _dot(h, wa_ref[...])
    zu = _dot(h, wu_ref[...])
    tm = za.shape[0]
    a = _causal_conv3(za, carry_a_ref[...], cwa_ref) + ba_ref[...]
    u = _causal_conv3(zu, carry_u_ref[...], cwu_ref) + bu_ref[...]
    carry_a_ref[...] = za[tm - SUBLANES:, :]
    carry_u_ref[...] = zu[tm - SUBLANES:, :]
    o_ref[...] = (_silu(a) * u).astype(o_ref.dtype)


def _ffn_up(h, w_up, conv_w, conv_b, seq):
    t, d = h.shape
    f = w_up.shape[1] // 2
    tm, tn = _tile(seq, TILE["proj_m"]), _tile(f, TILE["proj_n"])
    nj = f // tn
    cw = conv_w.shape[0]
    bias = conv_b.reshape(1, 2 * f)
    return pl.pallas_call(
        functools.partial(_ffn_up_kernel, tiles_per_seq=seq // tm),
        grid=(nj, t // tm),
        in_specs=[
            pl.BlockSpec((tm, d), lambda j, m: (m, 0)),
            pl.BlockSpec((d, tn), lambda j, m: (0, j)),
            pl.BlockSpec((d, tn), lambda j, m: (0, nj + j)),
            pl.BlockSpec((cw, tn), lambda j, m: (0, j)),
            pl.BlockSpec((cw, tn), lambda j, m: (0, nj + j)),
            pl.BlockSpec((1, tn), lambda j, m: (0, j)),
            pl.BlockSpec((1, tn), lambda j, m: (0, nj + j)),
        ],
        out_specs=pl.BlockSpec((tm, tn), lambda j, m: (m, j)),
        out_shape=jax.ShapeDtypeStruct((t, f), BF16),
        scratch_shapes=[pltpu.VMEM((SUBLANES, tn), F32), pltpu.VMEM((SUBLANES, tn), F32)],
        compiler_params=_params(("arbitrary", "arbitrary"), 48),
        name="ffn_up",
    )(h, w_up, w_up, conv_w, conv_w, bias, bias)


def _mm_ln_kernel(*refs, nk, alpha, emit_h):
    a_ref, w_ref, x_ref, gate_ref, lng_ref, lnb_ref = refs[:6]
    refs = refs[6:]
    if emit_h:
        sc_ref, sh_ref, xo_ref, h_ref = refs[:4]
        refs = refs[4:]
    else:
        xo_ref = refs[0]
        refs = refs[1:]

    def epilogue(y):
        z = alpha * x_ref[...] + gate_ref[0] * y
        mu = jnp.mean(z, axis=-1, keepdims=True)
        zc = z - mu
        var = jnp.mean(zc * zc, axis=-1, keepdims=True)
        xn = zc * lax.rsqrt(var + LN_EPS) * lng_ref[...] + lnb_ref[...]
        xo_ref[...] = xn
        if emit_h:
            h_ref[...] = (xn * (1.0 + sc_ref[0]) + sh_ref[0]).astype(h_ref.dtype)

    part = _dot(a_ref[...], w_ref[...])
    if nk == 1:
        epilogue(part)
        return
    acc_ref = refs[0]
    kk = pl.program_id(1)

    @pl.when(kk == 0)
    def _():
        acc_ref[...] = part

    @pl.when(kk > 0)
    def _():
        acc_ref[...] += part

    @pl.when(kk == nk - 1)
    def _():
        epilogue(acc_ref[...])


def _mm_ln(a, w, x2, mod_tab, layer, gate_chunk, ln_g, ln_b, next_mod, alpha, seq, name):
    t, kdim = a.shape
    d = w.shape[1]
    resident = kdim <= TILE["ln_k_resident_max"]
    tk = kdim if resident else _tile(kdim, TILE["ln_k"])
    tm = _tile(seq, TILE["ln_m_resident"] if resident else TILE["ln_m_ktiled"])
    nk = kdim // tk
    tiles_per_batch = seq // tm
    emit_h = next_mod is not None
    row = pl.BlockSpec((tm, d), lambda m, k: (m, 0))
    vec = pl.BlockSpec((1, d), lambda m, k: (0, 0))
    in_specs = [
        pl.BlockSpec((tm, tk), lambda m, k: (m, k)),
        pl.BlockSpec((tk, d), lambda m, k: (k, 0)),
        row,
        _mod_spec(d, layer, gate_chunk, tiles_per_batch),
        vec, vec,
    ]
    args = [a, w, x2, mod_tab, ln_g.reshape(1, d), ln_b.reshape(1, d)]
    out_specs = [row]
    out_shape = [jax.ShapeDtypeStruct((t, d), F32)]
    if emit_h:
        nl, sc_chunk, sh_chunk = next_mod
        in_specs += [_mod_spec(d, nl, sc_chunk, tiles_per_batch),
                     _mod_spec(d, nl, sh_chunk, tiles_per_batch)]
        args += [mod_tab, mod_tab]
        out_specs.append(row)
        out_shape.append(jax.ShapeDtypeStruct((t, d), BF16))
    scratch = [pltpu.VMEM((tm, d), F32)] if nk > 1 else []
    outs = pl.pallas_call(
        functools.partial(_mm_ln_kernel, nk=nk, alpha=alpha, emit_h=emit_h),
        grid=(t // tm, nk),
        in_specs=in_specs,
        out_specs=out_specs,
        out_shape=out_shape,
        scratch_shapes=scratch,
        compiler_params=_params(("arbitrary", "arbitrary"), 52),
        name=name,
    )(*args)
    return (outs[0], outs[1]) if emit_h else (outs[0], None)


def _pad_cols(w, n):
    return jnp.pad(w, ((0, 0), (0, n - w.shape[1])))


def _fox_mixer(h, wq, wk, wv, wg, wf, bf, batch, seq):
    t, d = h.shape
    heads = wf.shape[1]
    scale = (d // heads) ** -0.5
    q, k, v, g = _multi_proj(
        h, [wq.astype(BF16), wk.astype(BF16), wv.astype(BF16), wg.astype(BF16)],
        [lambda z: z * scale, lambda z: z, lambda z: z, jax.nn.sigmoid],
        [BF16, BF16, BF16, F32], "fox_proj")
    fcol, frow = _fox_cum(h, _pad_cols(wf, LANES).astype(BF16),
                          _pad_cols(bf.reshape(1, heads), LANES), batch, seq)
    shape3 = (batch, seq, d)
    o = _fox_attn(q.reshape(shape3), k.reshape(shape3), v.reshape(shape3), g.reshape(shape3),
                  fcol, frow, heads)
    return o.reshape(t, d)


def _gla_mixer(h, wq, wk, wv, wa1, wa2, ba, wr, norm_g, batch, seq):
    dv = norm_g.shape[-1]
    heads = wv.shape[1] // dv
    dk = wq.shape[1] // heads
    scale = dk ** -0.5
    q, k = _multi_proj(h, [wq.astype(BF16), wk.astype(BF16)],
                       [lambda z: z * scale, lambda z: z], [F32, F32], "gla_proj_qk")
    v, r = _multi_proj(h, [wv.astype(BF16), wr.astype(BF16)],
                       [lambda z: z, _silu], [BF16, F32], "gla_proj_vr")
    (a1,) = _multi_proj(h, [_pad_cols(wa1, LANES).astype(BF16)], [lambda z: z], [F32],
                        "gla_proj_a1")
    wa2_pad = jnp.pad(wa2, ((0, LANES - wa2.shape[0]), (0, 0))).astype(BF16)
    return _gla(q, k, v, a1, wa2_pad, ba.reshape(1, -1), norm_g.reshape(1, dv), r, batch, seq)


def kernel(x, c, ada_w, ada_b, ln1_g, ln1_b, ln2_g, ln2_b, fox_wq, fox_wk, fox_wv, fox_wg, fox_wf, fox_bf, fox_wo, gla_wq, gla_wk, gla_wv, gla_wa1, gla_wa2, gla_ba, gla_wr, gla_norm_g, gla_wo, conv_w_in, conv_w, conv_w_out, ffn_w_up, ffn_conv_w, ffn_conv_b, ffn_w_down):
    batch, seq, d = x.shape
    depth = ada_w.shape[0]
    assert batch <= MOD_ROWS
    alpha = (2 * depth) ** 0.25
    t = batch * seq

    c_pad = jnp.pad(c, ((0, MOD_ROWS - batch), (0, 0)))
    mod_tab = _ada_mod(c_pad, ada_w, ada_b).reshape(depth * MOD_ROWS * 6, 1, d)

    x2 = x.reshape(t, d)
    h = _modulate(x2, mod_tab, 0, seq)
    for i in range(depth):
        kind, j = i % 3, i // 3
        if kind == 0:
            y = _fox_mixer(h, fox_wq[j], fox_wk[j], fox_wv[j], fox_wg[j], fox_wf[j], fox_bf[j],
                           batch, seq)
            wo = fox_wo[j]
        elif kind == 1:
            y = _gla_mixer(h, gla_wq[j], gla_wk[j], gla_wv[j], gla_wa1[j], gla_wa2[j], gla_ba[j],
                           gla_wr[j], gla_norm_g[j], batch, seq)
            wo = gla_wo[j]
        else:
            y = _conv_mix(h, conv_w_in[j].astype(BF16), conv_w[j], seq)
            wo = conv_w_out[j]
        x2, h = _mm_ln(y, wo.astype(BF16), x2, mod_tab, i, 2, ln1_g[i], ln1_b[i], (i, 4, 3),
                       alpha, seq, "mixer_out_ln")
        act = _ffn_up(h, ffn_w_up[i].astype(BF16), ffn_conv_w[i], ffn_conv_b[i], seq)
        next_mod = (i + 1, 1, 0) if i + 1 < depth else None
        x2, h = _mm_ln(act, ffn_w_down[i].astype(BF16), x2, mod_tab, i, 5, ln2_g[i], ln2_b[i],
                       next_mod, alpha, seq, "ffn_down_ln")
    return x2.reshape(batch, seq, d)
```

```python
import functools

import numpy as np
import jax
import jax.numpy as jnp
from jax import lax
from jax.experimental import pallas as pl
from jax.experimental.pallas import tpu as pltpu

F32 = jnp.float32
BF16 = jnp.bfloat16

LANES = 128
SUBLANES = 8
MOD_ROWS = SUBLANES
MIB = 1 << 20

LOG2E = 1.4426950408889634
GLA_CHUNK = 64
GLA_GATE_TAU = 16.0
LN_EPS = 1e-5
RMS_EPS = 1e-5

TILE = dict(
    ada_n=1024,
    rows=512,
    proj_m=1024,
    proj_n=512,
    attn=512,
    gla_rows=4 * GLA_CHUNK,
    ln_m_resident=256,
    ln_m_ktiled=512,
    ln_k=512,
    ln_k_resident_max=2048,
)

_NT = (((1,), (1,)), ((), ()))
_TN = (((0,), (0,)), ((), ()))


def _params(semantics, vmem_mib):
    return pltpu.CompilerParams(dimension_semantics=semantics,
                                vmem_limit_bytes=vmem_mib * MIB)


def _tile(n, pref):
    t = min(n, pref)
    assert n % t == 0, (n, pref)
    return t


def _dot(a, b):
    return jnp.dot(a, b, preferred_element_type=F32)


def _log_sigmoid(z):
    return jnp.minimum(z, 0.0) - jnp.log1p(jnp.exp(-jnp.abs(z)))


def _silu(z):
    return z * jax.nn.sigmoid(z)


def _split3(x):
    hi = x.astype(BF16)
    r1 = x - hi.astype(F32)
    mid = r1.astype(BF16)
    lo = (r1 - mid.astype(F32)).astype(BF16)
    return hi, mid, lo


def _tril_mask(n):
    r = lax.broadcasted_iota(jnp.int32, (n, n), 0)
    c = lax.broadcasted_iota(jnp.int32, (n, n), 1)
    return c <= r


def _cumsum_rows(tril_bf16, x):
    hi, mid, lo = _split3(x)
    return (_dot(tril_bf16, lo) + _dot(tril_bf16, mid)) + _dot(tril_bf16, hi)


def _causal_conv3(x, prev, w_ref):
    rows = lax.broadcasted_iota(jnp.int32, x.shape, 0)
    p6 = prev[SUBLANES - 2:SUBLANES - 1, :]
    p7 = prev[SUBLANES - 1:SUBLANES, :]
    x1 = jnp.where(rows == 0, p7, pltpu.roll(x, 1, 0))
    x2 = jnp.where(rows == 0, p6, jnp.where(rows == 1, p7, pltpu.roll(x, 2, 0)))
    return x2 * w_ref[0:1, :] + x1 * w_ref[1:2, :] + x * w_ref[2:3, :]


def _ada_kernel(c_ref, w_ref, b_ref, o_ref):
    cond = _silu(c_ref[...]).astype(BF16)
    o_ref[0] = _dot(cond, w_ref[0].astype(BF16)) + b_ref[0]


def _ada_mod(c_pad, ada_w, ada_b):
    depth, d, n = ada_w.shape
    tn = _tile(n, TILE["ada_n"])
    return pl.pallas_call(
        _ada_kernel,
        grid=(depth, n // tn),
        in_specs=[
            pl.BlockSpec((MOD_ROWS, d), lambda i, j: (0, 0)),
            pl.BlockSpec((1, d, tn), lambda i, j: (i, 0, j)),
            pl.BlockSpec((1, 1, tn), lambda i, j: (i, 0, j)),
        ],
        out_specs=pl.BlockSpec((1, MOD_ROWS, tn), lambda i, j: (i, 0, j)),
        out_shape=jax.ShapeDtypeStruct((depth, MOD_ROWS, n), F32),
        compiler_params=_params(("arbitrary", "arbitrary"), 40),
        name="ada_mod",
    )(c_pad, ada_w, ada_b.reshape(depth, 1, n))


def _mod_spec(d, layer, chunk, rows_per_batch):
    def index(m, *_):
        return ((layer * MOD_ROWS + m // rows_per_batch) * 6 + chunk, 0, 0)
    return pl.BlockSpec((1, 1, d), index)


def _modulate_kernel(x_ref, sc_ref, sh_ref, h_ref):
    h_ref[...] = (x_ref[...] * (1.0 + sc_ref[0]) + sh_ref[0]).astype(BF16)


def _modulate(x2, mod_tab, layer, seq):
    t, d = x2.shape
    tm = _tile(seq, TILE["rows"])
    return pl.pallas_call(
        _modulate_kernel,
        grid=(t // tm,),
        in_specs=[
            pl.BlockSpec((tm, d), lambda m: (m, 0)),
            _mod_spec(d, layer, 1, seq // tm),
            _mod_spec(d, layer, 0, seq // tm),
        ],
        out_specs=pl.BlockSpec((tm, d), lambda m: (m, 0)),
        out_shape=jax.ShapeDtypeStruct((t, d), BF16),
        compiler_params=_params(("arbitrary",), 32),
        name="modulate",
    )(x2, mod_tab, mod_tab)


def _multi_proj_kernel(h_ref, *refs, posts):
    n = len(posts)
    h = h_ref[...]
    for w_ref, o_ref, post in zip(refs[:n], refs[n:], posts):
        o_ref[...] = post(_dot(h, w_ref[...])).astype(o_ref.dtype)


def _multi_proj(h, weights, posts, out_dtypes, name):
    t, d = h.shape
    n = weights[0].shape[1]
    tm, tn = _tile(t, TILE["proj_m"]), _tile(n, TILE["proj_n"])
    w_spec = pl.BlockSpec((d, tn), lambda j, m: (0, j))
    o_spec = pl.BlockSpec((tm, tn), lambda j, m: (m, j))
    return pl.pallas_call(
        functools.partial(_multi_proj_kernel, posts=tuple(posts)),
        grid=(n // tn, t // tm),
        in_specs=[pl.BlockSpec((tm, d), lambda j, m: (m, 0))] + [w_spec] * len(weights),
        out_specs=[o_spec] * len(weights),
        out_shape=[jax.ShapeDtypeStruct((t, n), dt) for dt in out_dtypes],
        compiler_params=_params(("arbitrary", "arbitrary"), 48),
        name=name,
    )(h, *weights)


def _proj_t_kernel(w_ref, h_ref, o_ref):
    o_ref[...] = lax.dot_general(w_ref[...], h_ref[...], _NT,
                                 preferred_element_type=F32).astype(o_ref.dtype)


def _proj_t(h, w_t, out_dtype, name):
    t, d = h.shape
    n = w_t.shape[0]
    tm, tn = _tile(t, TILE["proj_m"]), _tile(n, TILE["proj_n"])
    return pl.pallas_call(
        _proj_t_kernel,
        grid=(n // tn, t // tm),
        in_specs=[pl.BlockSpec((tn, d), lambda j, m: (j, 0)),
                  pl.BlockSpec((tm, d), lambda j, m: (m, 0))],
        out_specs=pl.BlockSpec((tn, tm), lambda j, m: (j, m)),
        out_shape=jax.ShapeDtypeStruct((n, t), out_dtype),
        compiler_params=_params(("arbitrary", "arbitrary"), 32),
        name=name,
    )(w_t, h)


def _fox_cum_kernel(h_ref, wf_ref, bf_ref, rep_ref, row_ref, carry_ref, *, heads):
    @pl.when(pl.program_id(1) == 0)
    def _():
        carry_ref[...] = jnp.zeros_like(carry_ref)

    log_f = _log_sigmoid(_dot(h_ref[...], wf_ref[...]) + bf_ref[...])
    ts = log_f.shape[0]
    tril = jnp.where(_tril_mask(ts), 1.0, 0.0).astype(BF16)
    cum = _cumsum_rows(tril, log_f) + carry_ref[...]
    carry_ref[...] = cum[ts - 1:ts, :]
    cum2 = cum * LOG2E
    for hd in range(heads):
        rep_ref[0, hd] = jnp.broadcast_to(cum2[:, hd:hd + 1], (ts, LANES))
    row_ref[0] = cum2.T


def _fox_cum(h, wf_pad, bf_pad, batch, seq, heads):
    t, d = h.shape
    ts = _tile(seq, TILE["rows"])
    ns = seq // ts
    return pl.pallas_call(
        functools.partial(_fox_cum_kernel, heads=heads),
        grid=(batch, ns),
        in_specs=[
            pl.BlockSpec((ts, d), lambda b, s: (b * ns + s, 0)),
            pl.BlockSpec((d, LANES), lambda b, s: (0, 0)),
            pl.BlockSpec((1, LANES), lambda b, s: (0, 0)),
        ],
        out_specs=[
            pl.BlockSpec((1, heads, ts, LANES), lambda b, s: (b, 0, s, 0)),
            pl.BlockSpec((1, LANES, ts), lambda b, s: (b, 0, s)),
        ],
        out_shape=[
            jax.ShapeDtypeStruct((batch, heads, seq, LANES), F32),
            jax.ShapeDtypeStruct((batch, LANES, seq), F32),
        ],
        scratch_shapes=[pltpu.VMEM((1, LANES), F32)],
        compiler_params=_params(("arbitrary", "arbitrary"), 32),
        name="fox_cum",
    )(h, wf_pad, bf_pad)


def _fox_attn_kernel(q_ref, k_ref, vt_ref, frow_ref, frep_ref, g_ref, o_ref,
                     m_ref, l_ref, acc_ref, s0_ref, s1_ref, smax0_ref, smax1_ref, *, tq):
    qi = pl.program_id(2)
    q = q_ref[...]
    f_t = frow_ref[pl.ds(qi, 1), :]
    m_ref[...] = jnp.full(m_ref.shape, -jnp.inf, F32)
    l_ref[...] = jnp.zeros(l_ref.shape, F32)
    acc_ref[...] = jnp.zeros(acc_ref.shape, F32)

    s_slots = (s0_ref, s1_ref)
    smax_slots = (smax0_ref, smax1_ref)

    def scores(j, slot, diagonal):
        start = pl.multiple_of(j * tq, tq)
        k = k_ref[pl.ds(start, tq), :]
        f_s = jnp.concatenate([frep_ref[pl.ds(start, tq), :]] * (tq // LANES), axis=1)
        s = lax.dot_general(k, q, _NT, preferred_element_type=F32) + (f_t - f_s)
        if diagonal:
            key = lax.broadcasted_iota(jnp.int32, (tq, tq), 0)
            query = lax.broadcasted_iota(jnp.int32, (tq, tq), 1)
            s = jnp.where(key <= query, s, -jnp.inf)
        s_slots[slot][...] = s
        smax_slots[slot][...] = jnp.max(s, axis=0, keepdims=True)

    def accumulate(j, slot):
        start = pl.multiple_of(j * tq, tq)
        vt = vt_ref[:, pl.ds(start, tq)]
        m_prev = m_ref[...]
        m_new = jnp.maximum(m_prev, smax_slots[slot][...])
        alpha = jnp.exp2(m_prev - m_new)
        p = jnp.exp2(s_slots[slot][...] - m_new)
        l_ref[...] = alpha * l_ref[...] + jnp.sum(p, axis=0, keepdims=True)
        acc_ref[...] = alpha * acc_ref[...] + _dot(vt, p.astype(BF16))
        m_ref[...] = m_new

    def by_parity(j, fn):
        for slot in (0, 1):
            pl.when(j % 2 == slot)(functools.partial(fn, slot))

    @pl.when(qi == 0)
    def _():
        scores(0, 0, True)

    @pl.when(qi > 0)
    def _():
        scores(0, 0, False)

    def body(j, carry):
        def step(slot):
            scores(j + 1, 1 - slot, False)
            accumulate(j, slot)
        by_parity(j, step)
        return carry

    lax.fori_loop(0, qi - 1, body, 0)

    def last_step(slot):
        scores(qi, 1 - slot, True)
        accumulate(qi - 1, slot)

    @pl.when(qi > 0)
    def _():
        by_parity(qi - 1, last_step)

    by_parity(qi, lambda slot: accumulate(qi, slot))
    o = (acc_ref[...] / l_ref[...]).T
    o_ref[...] = (o * g_ref[...]).astype(o_ref.dtype)


def _fox_attn(q, k, vt, g, frow, frep, heads):
    batch, seq, d = q.shape
    dh = d // heads
    assert dh == LANES and heads <= LANES
    tq = _tile(seq, TILE["attn"])
    nq = seq // tq
    frow4 = frow.reshape(batch, LANES, nq, tq)
    qspec = pl.BlockSpec((None, tq, dh), lambda b, h, i: (b, i, h))
    return pl.pallas_call(
        functools.partial(_fox_attn_kernel, tq=tq),
        grid=(batch, heads, nq),
        in_specs=[
            qspec,
            pl.BlockSpec((None, seq, dh), lambda b, h, i: (b, 0, h)),
            pl.BlockSpec((dh, seq), lambda b, h, i: (h, b)),
            pl.BlockSpec((None, None, nq, tq), lambda b, h, i: (b, h, 0, 0)),
            pl.BlockSpec((None, None, seq, LANES), lambda b, h, i: (b, h, 0, 0)),
            qspec,
        ],
        out_specs=qspec,
        out_shape=jax.ShapeDtypeStruct((batch, seq, d), BF16),
        scratch_shapes=[
            pltpu.VMEM((1, tq), F32),
            pltpu.VMEM((1, tq), F32),
            pltpu.VMEM((dh, tq), F32),
            pltpu.VMEM((tq, tq), F32),
            pltpu.VMEM((tq, tq), F32),
            pltpu.VMEM((1, tq), F32),
            pltpu.VMEM((1, tq), F32),
        ],
        compiler_params=_params(("arbitrary", "arbitrary", "arbitrary"), 32),
        name="fox_attn",
    )(q, k, vt, frow4, frep, g)


def _gla_kernel(q_ref, k_ref, v_ref, a1_ref, wa2_ref, ba_ref, ng_ref, r_ref, o_ref,
                state_ref, *, n_sub):
    @pl.when(pl.program_id(2) == 0)
    def _():
        state_ref[...] = jnp.zeros_like(state_ref)

    c = GLA_CHUNK
    causal = _tril_mask(c)
    tril = jnp.where(causal, 1.0, 0.0).astype(BF16)
    ones = jnp.ones((c, LANES), BF16)
    for sub in range(n_sub):
        rows = slice(sub * c, (sub + 1) * c)
        gate = _dot(a1_ref[rows, :].astype(BF16), wa2_ref[...]) + ba_ref[...]
        log_a = _log_sigmoid(gate) / GLA_GATE_TAU
        la_hi, la_mid, la_lo = _split3(log_a)
        cb = (_dot(tril, la_lo) + _dot(tril, la_mid)) + _dot(tril, la_hi)
        cb_last = cb[c - 1:c, :]
        decay_col = ((lax.dot_general(la_lo, ones, _TN, preferred_element_type=F32)
                      + lax.dot_general(la_mid, ones, _TN, preferred_element_type=F32))
                     + lax.dot_general(la_hi, ones, _TN, preferred_element_type=F32))
        q = q_ref[rows, :]
        k = k_ref[rows, :]
        v = v_ref[rows, :]
        q_dec = (q * jnp.exp(cb)).astype(BF16)
        k_inv = (k * jnp.exp(-cb)).astype(BF16)
        k_dec = (k * jnp.exp(cb_last - cb)).astype(BF16)
        att = lax.dot_general(q_dec, k_inv, _NT, preferred_element_type=F32)
        att = jnp.where(causal, att, 0.0)
        state = state_ref[...]
        o = _dot(att.astype(BF16), v) + _dot(q_dec, state.astype(BF16))
        state_ref[...] = (jnp.exp(decay_col[:, 0:1]) * state
                          + lax.dot_general(k_dec, v, _TN, preferred_element_type=F32))
        o = o * lax.rsqrt(jnp.mean(o * o, axis=-1, keepdims=True) + RMS_EPS) * ng_ref[...]
        o_ref[rows, :] = (o * r_ref[rows, :]).astype(o_ref.dtype)


def _gla(q, k, v, a1, wa2_pad, ba, norm_g, r, batch, seq):
    t, kdim = q.shape
    vdim = v.shape[1]
    dv = norm_g.shape[-1]
    heads = vdim // dv
    dk = kdim // heads
    rows = _tile(seq, TILE["gla_rows"])
    ns = seq // rows

    def row_spec(width):
        return pl.BlockSpec((rows, width), lambda b, h, s: (b * ns + s, h))

    return pl.pallas_call(
        functools.partial(_gla_kernel, n_sub=rows // GLA_CHUNK),
        grid=(batch, heads, ns),
        in_specs=[
            row_spec(dk), row_spec(dk), row_spec(dv),
            pl.BlockSpec((rows, LANES), lambda b, h, s: (b * ns + s, 0)),
            pl.BlockSpec((LANES, dk), lambda b, h, s: (0, h)),
            pl.BlockSpec((1, dk), lambda b, h, s: (0, h)),
            pl.BlockSpec((1, dv), lambda b, h, s: (0, 0)),
            row_spec(dv),
        ],
        out_specs=row_spec(dv),
        out_shape=jax.ShapeDtypeStruct((t, vdim), BF16),
        scratch_shapes=[pltpu.VMEM((dk, dv), F32)],
        compiler_params=_params(("arbitrary", "arbitrary", "arbitrary"), 32),
        name="gla_chunk",
    )(q, k, v, a1, wa2_pad, ba, norm_g, r)


def _conv_mix_kernel(h_ref, wb_ref, wc_ref, wu_ref, cw_ref, o_ref, carry_ref, *, tiles_per_seq):
    @pl.when(pl.program_id(1) % tiles_per_seq == 0)
    def _():
        carry_ref[...] = jnp.zeros_like(carry_ref)

    h = h_ref[...]
    cu = _dot(h, wc_ref[...]) * _dot(h, wu_ref[...])
    y = _causal_conv3(cu, carry_ref[...], cw_ref)
    carry_ref[...] = cu[cu.shape[0] - SUBLANES:, :]
    o_ref[...] = (_dot(h, wb_ref[...]) * y).astype(o_ref.dtype)


def _conv_mix(h, w_in, conv_w, seq):
    t, d = h.shape
    n = w_in.shape[1] // 3
    tm, tn = _tile(seq, TILE["proj_m"]), _tile(n, TILE["proj_n"])
    nj = n // tn
    return pl.pallas_call(
        functools.partial(_conv_mix_kernel, tiles_per_seq=seq // tm),
        grid=(nj, t // tm),
        in_specs=[
            pl.BlockSpec((tm, d), lambda j, m: (m, 0)),
            pl.BlockSpec((d, tn), lambda j, m: (0, j)),
            pl.BlockSpec((d, tn), lambda j, m: (0, nj + j)),
            pl.BlockSpec((d, tn), lambda j, m: (0, 2 * nj + j)),
            pl.BlockSpec((conv_w.shape[0], tn), lambda j, m: (0, j)),
        ],
        out_specs=pl.BlockSpec((tm, tn), lambda j, m: (m, j)),
        out_shape=jax.ShapeDtypeStruct((t, n), BF16),
        scratch_shapes=[pltpu.VMEM((SUBLANES, tn), F32)],
        compiler_params=_params(("arbitrary", "arbitrary"), 48),
        name="conv_mix",
    )(h, w_in, w_in, w_in, conv_w)


def _ffn_up_kernel(h_ref, wa_ref, wu_ref, cwa_ref, cwu_ref, ba_ref, bu_ref, o_ref,
                   carry_a_ref, carry_u_ref, *, tiles_per_seq):
    @pl.when(pl.program_id(1) % tiles_per_seq == 0)
    def _():
        carry_a_ref[...] = jnp.zeros_like(carry_a_ref)
        carry_u_ref[...] = jnp.zeros_like(carry_u_ref)

    h = h_ref[...]
    za = _dot(h, wa_ref[...])
    zu = _dot(h, wu_ref[...])
    tm = za.shape[0]
    a = _causal_conv3(za, carry_a_ref[...], cwa_ref) + ba_ref[...]
    u = _causal_conv3(zu, carry_u_ref[...], cwu_ref) + bu_ref[...]
    carry_a_ref[...] = za[tm - SUBLANES:, :]
    carry_u_ref[...] = zu[tm - SUBLANES:, :]
    o_ref[...] = (_silu(a) * u).astype(o_ref.dtype)


def _ffn_up(h, w_up, conv_w, conv_b, seq):
    t, d = h.shape
    f = w_up.shape[1] // 2
    tm, tn = _tile(seq, TILE["proj_m"]), _tile(f, TILE["proj_n"])
    nj = f // tn
    cw = conv_w.shape[0]
    bias = conv_b.reshape(1, 2 * f)
    return pl.pallas_call(
        functools.partial(_ffn_up_kernel, tiles_per_seq=seq // tm),
        grid=(nj, t // tm),
        in_specs=[
            pl.BlockSpec((tm, d), lambda j, m: (m, 0)),
            pl.BlockSpec((d, tn), lambda j, m: (0, j)),
            pl.BlockSpec((d, tn), lambda j, m: (0, nj + j)),
            pl.BlockSpec((cw, tn), lambda j, m: (0, j)),
            pl.BlockSpec((cw, tn), lambda j, m: (0, nj + j)),
            pl.BlockSpec((1, tn), lambda j, m: (0, j)),
            pl.BlockSpec((1, tn), lambda j, m: (0, nj + j)),
        ],
        out_specs=pl.BlockSpec((tm, tn), lambda j, m: (m, j)),
        out_shape=jax.ShapeDtypeStruct((t, f), BF16),
        scratch_shapes=[pltpu.VMEM((SUBLANES, tn), F32), pltpu.VMEM((SUBLANES, tn), F32)],
        compiler_params=_params(("arbitrary", "arbitrary"), 48),
        name="ffn_up",
    )(h, w_up, w_up, conv_w, conv_w, bias, bias)


def _mm_ln_kernel(*refs, nk, alpha, emit_h):
    a_ref, w_ref, x_ref, gate_ref, lng_ref, lnb_ref = refs[:6]
    refs = refs[6:]
    if emit_h:
        sc_ref, sh_ref, xo_ref, h_ref = refs[:4]
        refs = refs[4:]
    else:
        xo_ref = refs[0]
        refs = refs[1:]

    def epilogue(y):
        z = alpha * x_ref[...] + gate_ref[0] * y
        mu = jnp.mean(z, axis=-1, keepdims=True)
        zc = z - mu
        var = jnp.mean(zc * zc, axis=-1, keepdims=True)
        xn = zc * lax.rsqrt(var + LN_EPS) * lng_ref[...] + lnb_ref[...]
        xo_ref[...] = xn
        if emit_h:
            h_ref[...] = (xn * (1.0 + sc_ref[0]) + sh_ref[0]).astype(h_ref.dtype)

    if nk == 1:
        epilogue(_dot(a_ref[...], w_ref[...]))
        return
    acc_ref = refs[0]
    kk = pl.program_id(1)

    @pl.when(kk == 0)
    def _():
        acc_ref[...] = jnp.zeros_like(acc_ref)

    acc_ref[...] += _dot(a_ref[...], w_ref[...])

    @pl.when(kk == nk - 1)
    def _():
        epilogue(acc_ref[...])


def _mm_ln(a, w, x2, mod_tab, layer, gate_chunk, ln_g, ln_b, next_mod, alpha, seq, name):
    t, kdim = a.shape
    d = w.shape[1]
    resident = kdim <= TILE["ln_k_resident_max"]
    tk = kdim if resident else _tile(kdim, TILE["ln_k"])
    tm = _tile(seq, TILE["ln_m_resident"] if resident else TILE["ln_m_ktiled"])
    nk = kdim // tk
    tiles_per_batch = seq // tm
    emit_h = next_mod is not None
    row = pl.BlockSpec((tm, d), lambda m, k: (m, 0))
    vec = pl.BlockSpec((1, d), lambda m, k: (0, 0))
    in_specs = [
        pl.BlockSpec((tm, tk), lambda m, k: (m, k)),
        pl.BlockSpec((tk, d), lambda m, k: (k, 0)),
        row,
        _mod_spec(d, layer, gate_chunk, tiles_per_batch),
        vec, vec,
    ]
    args = [a, w, x2, mod_tab, ln_g.reshape(1, d), ln_b.reshape(1, d)]
    out_specs = [row]
    out_shape = [jax.ShapeDtypeStruct((t, d), F32)]
    if emit_h:
        nl, sc_chunk, sh_chunk = next_mod
        in_specs += [_mod_spec(d, nl, sc_chunk, tiles_per_batch),
                     _mod_spec(d, nl, sh_chunk, tiles_per_batch)]
        args += [mod_tab, mod_tab]
        out_specs.append(row)
        out_shape.append(jax.ShapeDtypeStruct((t, d), BF16))
    scratch = [pltpu.VMEM((tm, d), F32)] if nk > 1 else []
    outs = pl.pallas_call(
        functools.partial(_mm_ln_kernel, nk=nk, alpha=alpha, emit_h=emit_h),
        grid=(t // tm, nk),
        in_specs=in_specs,
        out_specs=out_specs,
        out_shape=out_shape,
        scratch_shapes=scratch,
        compiler_params=_params(("arbitrary", "arbitrary"), 52),
        name=name,
    )(*args)
    return (outs[0], outs[1]) if emit_h else (outs[0], None)


def _pad_cols(w, n):
    return jnp.pad(w, ((0, 0), (0, n - w.shape[1])))


def _fox_mixer(h, wq, wk, wv, wg, wf, bf, batch, seq):
    t, d = h.shape
    heads = wf.shape[1]
    scale = (d // heads) ** -0.5 * LOG2E
    q, k, g = _multi_proj(
        h, [wq.astype(BF16), wk.astype(BF16), wg.astype(BF16)],
        [lambda z: z * scale, lambda z: z, jax.nn.sigmoid],
        [BF16, BF16, F32], "fox_proj")
    vt = _proj_t(h, wv.T.astype(BF16), BF16, "fox_proj_vt")
    frep, frow = _fox_cum(h, _pad_cols(wf, LANES).astype(BF16),
                          _pad_cols(bf.reshape(1, heads), LANES), batch, seq, heads)
    shape3 = (batch, seq, d)
    o = _fox_attn(q.reshape(shape3), k.reshape(shape3), vt, g.reshape(shape3),
                  frow, frep, heads)
    return o.reshape(t, d)


def _gla_mixer(h, wq, wk, wv, wa1, wa2, ba, wr, norm_g, batch, seq):
    dv = norm_g.shape[-1]
    heads = wv.shape[1] // dv
    dk = wq.shape[1] // heads
    scale = dk ** -0.5
    q, k = _multi_proj(h, [wq.astype(BF16), wk.astype(BF16)],
                       [lambda z: z * scale, lambda z: z], [F32, F32], "gla_proj_qk")
    v, r = _multi_proj(h, [wv.astype(BF16), wr.astype(BF16)],
                       [lambda z: z, _silu], [BF16, F32], "gla_proj_vr")
    (a1,) = _multi_proj(h, [_pad_cols(wa1, LANES).astype(BF16)], [lambda z: z], [F32],
                        "gla_proj_a1")
    wa2_pad = jnp.pad(wa2, ((0, LANES - wa2.shape[0]), (0, 0))).astype(BF16)
    return _gla(q, k, v, a1, wa2_pad, ba.reshape(1, -1), norm_g.reshape(1, dv), r, batch, seq)


def kernel(x, c, ada_w, ada_b, ln1_g, ln1_b, ln2_g, ln2_b, fox_wq, fox_wk, fox_wv, fox_wg, fox_wf, fox_bf, fox_wo, gla_wq, gla_wk, gla_wv, gla_wa1, gla_wa2, gla_ba, gla_wr, gla_norm_g, gla_wo, conv_w_in, conv_w, conv_w_out, ffn_w_up, ffn_conv_w, ffn_conv_b, ffn_w_down):
    batch, seq, d = x.shape
    depth = ada_w.shape[0]
    assert batch <= MOD_ROWS
    alpha = (2 * depth) ** 0.25
    t = batch * seq

    c_pad = jnp.pad(c, ((0, MOD_ROWS - batch), (0, 0)))
    mod_tab = _ada_mod(c_pad, ada_w, ada_b).reshape(depth * MOD_ROWS * 6, 1, d)

    x2 = x.reshape(t, d)
    h = _modulate(x2, mod_tab, 0, seq)
    for i in range(depth):
        kind, j = i % 3, i // 3
        if kind == 0:
            y = _fox_mixer(h, fox_wq[j], fox_wk[j], fox_wv[j], fox_wg[j], fox_wf[j], fox_bf[j],
                           batch, seq)
            wo = fox_wo[j]
        elif kind == 1:
            y = _gla_mixer(h, gla_wq[j], gla_wk[j], gla_wv[j], gla_wa1[j], gla_wa2[j], gla_ba[j],
                           gla_wr[j], gla_norm_g[j], batch, seq)
            wo = gla_wo[j]
        else:
            y = _conv_mix(h, conv_w_in[j].astype(BF16), conv_w[j], seq)
            wo = conv_w_out[j]
        x2, h = _mm_ln(y, wo.astype(BF16), x2, mod_tab, i, 2, ln1_g[i], ln1_b[i], (i, 4, 3),
                       alpha, seq, "mixer_out_ln")
        act = _ffn_up(h, ffn_w_up[i].astype(BF16), ffn_conv_w[i], ffn_conv_b[i], seq)
        next_mod = (i + 1, 1, 0) if i + 1 < depth else None
        x2, h = _mm_ln(act, ffn_w_down[i].astype(BF16), x2, mod_tab, i, 5, ln2_g[i], ln2_b[i],
                       next_mod, alpha, seq, "ffn_down_ln")
    return x2.reshape(batch, seq, d)
```

```python
import functools

import numpy as np
import jax
import jax.numpy as jnp
from jax import lax
from jax.experimental import pallas as pl
from jax.experimental.pallas import tpu as pltpu

F32 = jnp.float32
BF16 = jnp.bfloat16

LANES = 128
SUBLANES = 8
MOD_ROWS = SUBLANES
MIB = 1 << 20

LOG2E = 1.4426950408889634
GLA_CHUNK = 64
GLA_GATE_TAU = 16.0
LN_EPS = 1e-5
RMS_EPS = 1e-5

TILE = dict(
    ada_n=1024,
    rows=512,
    proj_m=1024,
    proj_n=512,
    attn=512,
    gla_rows=16 * GLA_CHUNK,
    ln_m=512,
    ln_m_big_k=256,
    ln_k_small=2048,
)

_NT = (((1,), (1,)), ((), ()))
_TN = (((0,), (0,)), ((), ()))


def _params(semantics, vmem_mib):
    return pltpu.CompilerParams(dimension_semantics=semantics,
                                vmem_limit_bytes=vmem_mib * MIB)


def _tile(n, pref):
    t = min(n, pref)
    assert n % t == 0, (n, pref)
    return t


def _dot(a, b):
    return jnp.dot(a, b, preferred_element_type=F32)


def _log_sigmoid(z):
    return jnp.minimum(z, 0.0) - jnp.log1p(jnp.exp(-jnp.abs(z)))


def _silu(z):
    return z * jax.nn.sigmoid(z)


def _split3(x):
    hi = x.astype(BF16)
    r1 = x - hi.astype(F32)
    mid = r1.astype(BF16)
    lo = (r1 - mid.astype(F32)).astype(BF16)
    return hi, mid, lo


def _tril_mask(n):
    r = lax.broadcasted_iota(jnp.int32, (n, n), 0)
    c = lax.broadcasted_iota(jnp.int32, (n, n), 1)
    return c <= r


def _cumsum_rows(tril_bf16, x):
    hi, mid, lo = _split3(x)
    return (_dot(tril_bf16, lo) + _dot(tril_bf16, mid)) + _dot(tril_bf16, hi)


def _causal_conv3(x, prev, w_ref):
    rows = lax.broadcasted_iota(jnp.int32, x.shape, 0)
    p6 = prev[SUBLANES - 2:SUBLANES - 1, :]
    p7 = prev[SUBLANES - 1:SUBLANES, :]
    x1 = jnp.where(rows == 0, p7, pltpu.roll(x, 1, 0))
    x2 = jnp.where(rows == 0, p6, jnp.where(rows == 1, p7, pltpu.roll(x, 2, 0)))
    return x2 * w_ref[0:1, :] + x1 * w_ref[1:2, :] + x * w_ref[2:3, :]


def _ada_kernel(c_ref, w_ref, b_ref, o_ref):
    cond = _silu(c_ref[...]).astype(BF16)
    o_ref[0] = _dot(cond, w_ref[0].astype(BF16)) + b_ref[0]


def _ada_mod(c_pad, ada_w, ada_b):
    depth, d, n = ada_w.shape
    tn = _tile(n, TILE["ada_n"])
    return pl.pallas_call(
        _ada_kernel,
        grid=(depth, n // tn),
        in_specs=[
            pl.BlockSpec((MOD_ROWS, d), lambda i, j: (0, 0)),
            pl.BlockSpec((1, d, tn), lambda i, j: (i, 0, j)),
            pl.BlockSpec((1, 1, tn), lambda i, j: (i, 0, j)),
        ],
        out_specs=pl.BlockSpec((1, MOD_ROWS, tn), lambda i, j: (i, 0, j)),
        out_shape=jax.ShapeDtypeStruct((depth, MOD_ROWS, n), F32),
        compiler_params=_params(("arbitrary", "arbitrary"), 40),
        name="ada_mod",
    )(c_pad, ada_w, ada_b.reshape(depth, 1, n))


def _mod_spec(d, layer, chunk, tiles_per_batch, lag=0):
    def index(m, *_):
        tile = jnp.maximum(m - lag, 0) if lag else m
        return ((layer * MOD_ROWS + tile // tiles_per_batch) * 6 + chunk, 0, 0)
    return pl.BlockSpec((1, 1, d), index)


def _modulate_kernel(x_ref, sc_ref, sh_ref, h_ref):
    h_ref[...] = (x_ref[...] * (1.0 + sc_ref[0]) + sh_ref[0]).astype(BF16)


def _modulate(x2, mod_tab, layer, seq):
    t, d = x2.shape
    tm = _tile(seq, TILE["rows"])
    return pl.pallas_call(
        _modulate_kernel,
        grid=(t // tm,),
        in_specs=[
            pl.BlockSpec((tm, d), lambda m: (m, 0)),
            _mod_spec(d, layer, 1, seq // tm),
            _mod_spec(d, layer, 0, seq // tm),
        ],
        out_specs=pl.BlockSpec((tm, d), lambda m: (m, 0)),
        out_shape=jax.ShapeDtypeStruct((t, d), BF16),
        compiler_params=_params(("arbitrary",), 32),
        name="modulate",
    )(x2, mod_tab, mod_tab)


def _multi_proj_kernel(h_ref, *refs, posts):
    n = len(posts)
    h = h_ref[...]
    for w_ref, o_ref, post in zip(refs[:n], refs[n:], posts):
        o_ref[...] = post(_dot(h, w_ref[...])).astype(o_ref.dtype)


def _multi_proj(h, weights, posts, out_dtypes, name):
    t, d = h.shape
    n = weights[0].shape[1]
    tm, tn = _tile(t, TILE["proj_m"]), _tile(n, TILE["proj_n"])
    w_spec = pl.BlockSpec((d, tn), lambda j, m: (0, j))
    o_spec = pl.BlockSpec((tm, tn), lambda j, m: (m, j))
    return pl.pallas_call(
        functools.partial(_multi_proj_kernel, posts=tuple(posts)),
        grid=(n // tn, t // tm),
        in_specs=[pl.BlockSpec((tm, d), lambda j, m: (m, 0))] + [w_spec] * len(weights),
        out_specs=[o_spec] * len(weights),
        out_shape=[jax.ShapeDtypeStruct((t, n), dt) for dt in out_dtypes],
        compiler_params=_params(("arbitrary", "arbitrary"), 48),
        name=name,
    )(h, *weights)


def _proj_t_kernel(w_ref, h_ref, o_ref):
    o_ref[...] = lax.dot_general(w_ref[...], h_ref[...], _NT,
                                 preferred_element_type=F32).astype(o_ref.dtype)


def _proj_t(h, w_t, out_dtype, name):
    t, d = h.shape
    n = w_t.shape[0]
    tm, tn = _tile(t, TILE["proj_m"]), _tile(n, TILE["proj_n"])
    return pl.pallas_call(
        _proj_t_kernel,
        grid=(n // tn, t // tm),
        in_specs=[pl.BlockSpec((tn, d), lambda j, m: (j, 0)),
                  pl.BlockSpec((tm, d), lambda j, m: (m, 0))],
        out_specs=pl.BlockSpec((tn, tm), lambda j, m: (j, m)),
        out_shape=jax.ShapeDtypeStruct((n, t), out_dtype),
        compiler_params=_params(("arbitrary", "arbitrary"), 32),
        name=name,
    )(w_t, h)


def _fox_cum_kernel(h_ref, wf_ref, bf_ref, pq_ref, pk_ref, cq_ref, ck_ref, qb_ref, kb_ref,
                    carry_ref):
    @pl.when(pl.program_id(1) == 0)
    def _():
        carry_ref[...] = jnp.zeros_like(carry_ref)

    log_f = _log_sigmoid(_dot(h_ref[...], wf_ref[...]) + bf_ref[...])
    ts = log_f.shape[0]
    tril = jnp.where(_tril_mask(ts), 1.0, 0.0).astype(BF16)
    cum = _cumsum_rows(tril, log_f) + carry_ref[...]
    carry_ref[...] = cum[ts - 1:ts, :]
    parts = jnp.concatenate(_split3(cum * LOG2E), axis=1)
    qb_ref[...] = (_dot(parts, pq_ref[...]) + cq_ref[...]).astype(qb_ref.dtype)
    kb_ref[...] = (_dot(parts, pk_ref[...]) + ck_ref[...]).astype(kb_ref.dtype)


def _bias_placement(heads):
    pq = np.zeros((3 * LANES, heads * LANES), np.float32)
    pk = np.zeros((3 * LANES, heads * LANES), np.float32)
    cq = np.zeros((1, heads * LANES), np.float32)
    ck = np.zeros((1, heads * LANES), np.float32)
    for hd in range(heads):
        for term in range(3):
            pq[term * LANES + hd, hd * LANES + term] = 1.0
            pk[term * LANES + hd, hd * LANES + 3 + term] = -1.0
        cq[0, hd * LANES + 3:hd * LANES + 6] = 1.0
        ck[0, hd * LANES:hd * LANES + 3] = 1.0
    return jnp.asarray(pq, BF16), jnp.asarray(pk, BF16), jnp.asarray(cq), jnp.asarray(ck)


def _fox_cum(h, wf_pad, bf_pad, batch, seq, heads):
    t, d = h.shape
    ts = _tile(seq, TILE["rows"])
    ns = seq // ts
    width = heads * LANES

    def const(b, s):
        return (0, 0)

    out_spec = pl.BlockSpec((ts, width), lambda b, s: (b * ns + s, 0))
    return pl.pallas_call(
        _fox_cum_kernel,
        grid=(batch, ns),
        in_specs=[
            pl.BlockSpec((ts, d), lambda b, s: (b * ns + s, 0)),
            pl.BlockSpec((d, LANES), const),
            pl.BlockSpec((1, LANES), const),
            pl.BlockSpec((3 * LANES, width), const),
            pl.BlockSpec((3 * LANES, width), const),
            pl.BlockSpec((1, width), const),
            pl.BlockSpec((1, width), const),
        ],
        out_specs=[out_spec, out_spec],
        out_shape=[jax.ShapeDtypeStruct((t, width), BF16)] * 2,
        scratch_shapes=[pltpu.VMEM((1, LANES), F32)],
        compiler_params=_params(("arbitrary", "arbitrary"), 32),
        name="fox_cum",
    )(h, wf_pad, bf_pad, *_bias_placement(heads))


def _fox_attn_kernel(q_ref, qb_ref, k_ref, kb_ref, vt_ref, g_ref, o_ref,
                     m_ref, l_ref, acc_ref, s0_ref, s1_ref, smax0_ref, smax1_ref, *, tq):
    qi = pl.program_id(2)
    q = jnp.concatenate([q_ref[...], qb_ref[...]], axis=1)
    m_ref[...] = jnp.full(m_ref.shape, -jnp.inf, F32)
    l_ref[...] = jnp.zeros(l_ref.shape, F32)
    acc_ref[...] = jnp.zeros(acc_ref.shape, F32)

    s_slots = (s0_ref, s1_ref)
    smax_slots = (smax0_ref, smax1_ref)

    def scores(j, slot, diagonal):
        rows = pl.ds(pl.multiple_of(j * tq, tq), tq)
        k = jnp.concatenate([k_ref[rows, :], kb_ref[rows, :]], axis=1)
        s = lax.dot_general(k, q, _NT, preferred_element_type=F32)
        if diagonal:
            key = lax.broadcasted_iota(jnp.int32, (tq, tq), 0)
            query = lax.broadcasted_iota(jnp.int32, (tq, tq), 1)
            s = jnp.where(key <= query, s, -jnp.inf)
        s_slots[slot][...] = s
        smax_slots[slot][...] = jnp.max(s, axis=0, keepdims=True)

    def accumulate(j, slot):
        start = pl.multiple_of(j * tq, tq)
        vt = vt_ref[:, pl.ds(start, tq)]
        m_prev = m_ref[...]
        m_new = jnp.maximum(m_prev, smax_slots[slot][...])
        alpha = jnp.exp2(m_prev - m_new)
        p = jnp.exp2(s_slots[slot][...] - m_new)
        l_ref[...] = alpha * l_ref[...] + jnp.sum(p, axis=0, keepdims=True)
        acc_ref[...] = alpha * acc_ref[...] + _dot(vt, p.astype(BF16))
        m_ref[...] = m_new

    def by_parity(j, fn):
        for slot in (0, 1):
            pl.when(j % 2 == slot)(functools.partial(fn, slot))

    @pl.when(qi == 0)
    def _():
        scores(0, 0, True)

    @pl.when(qi > 0)
    def _():
        scores(0, 0, False)

    def body(j, carry):
        def step(slot):
            scores(j + 1, 1 - slot, False)
            accumulate(j, slot)
        by_parity(j, step)
        return carry

    lax.fori_loop(0, qi - 1, body, 0)

    def last_step(slot):
        scores(qi, 1 - slot, True)
        accumulate(qi - 1, slot)

    @pl.when(qi > 0)
    def _():
        by_parity(qi - 1, last_step)

    by_parity(qi, lambda slot: accumulate(qi, slot))
    o = (acc_ref[...] / l_ref[...]).T
    o_ref[...] = (o * g_ref[...]).astype(o_ref.dtype)


def _fox_attn(q, qb, k, kb, vt, g, heads):
    batch, seq, d = q.shape
    dh = d // heads
    assert dh == LANES and heads <= LANES
    tq = _tile(seq, TILE["attn"])
    nq = seq // tq
    qspec = pl.BlockSpec((None, tq, dh), lambda b, h, i: (b, i, h))
    kspec = pl.BlockSpec((None, seq, dh), lambda b, h, i: (b, 0, h))
    return pl.pallas_call(
        functools.partial(_fox_attn_kernel, tq=tq),
        grid=(batch, heads, nq),
        in_specs=[
            qspec, qspec, kspec, kspec,
            pl.BlockSpec((dh, seq), lambda b, h, i: (h, b)),
            qspec,
        ],
        out_specs=qspec,
        out_shape=jax.ShapeDtypeStruct((batch, seq, d), BF16),
        scratch_shapes=[
            pltpu.VMEM((1, tq), F32),
            pltpu.VMEM((1, tq), F32),
            pltpu.VMEM((dh, tq), F32),
            pltpu.VMEM((tq, tq), F32),
            pltpu.VMEM((tq, tq), F32),
            pltpu.VMEM((1, tq), F32),
            pltpu.VMEM((1, tq), F32),
        ],
        compiler_params=_params(("arbitrary", "arbitrary", "arbitrary"), 32),
        name="fox_attn",
    )(q, qb, k, kb, vt, g)


def _gla_kernel(q_ref, k_ref, v_ref, a1_ref, wa2_ref, ba_ref, ng_ref, r_ref, o_ref,
                state_ref, *, n_sub):
    @pl.when(pl.program_id(2) == 0)
    def _():
        state_ref[...] = jnp.zeros_like(state_ref)

    c = GLA_CHUNK
    causal = _tril_mask(c)
    tril = jnp.where(causal, 1.0, 0.0).astype(BF16)
    ones = jnp.ones((c, LANES), BF16)
    for sub in range(n_sub):
        rows = slice(sub * c, (sub + 1) * c)
        gate = _dot(a1_ref[rows, :].astype(BF16), wa2_ref[...]) + ba_ref[...]
        log_a = _log_sigmoid(gate) / GLA_GATE_TAU
        la_hi, la_mid, la_lo = _split3(log_a)
        cb = (_dot(tril, la_lo) + _dot(tril, la_mid)) + _dot(tril, la_hi)
        cb_last = cb[c - 1:c, :]
        decay_col = ((lax.dot_general(la_lo, ones, _TN, preferred_element_type=F32)
                      + lax.dot_general(la_mid, ones, _TN, preferred_element_type=F32))
                     + lax.dot_general(la_hi, ones, _TN, preferred_element_type=F32))
        q = q_ref[rows, :]
        k = k_ref[rows, :]
        v = v_ref[rows, :]
        q_dec = (q * jnp.exp(cb)).astype(BF16)
        k_inv = (k * jnp.exp(-cb)).astype(BF16)
        k_dec = (k * jnp.exp(cb_last - cb)).astype(BF16)
        att = lax.dot_general(q_dec, k_inv, _NT, preferred_element_type=F32)
        att = jnp.where(causal, att, 0.0)
        state = state_ref[...]
        o = _dot(att.astype(BF16), v) + _dot(q_dec, state.astype(BF16))
        state_ref[...] = (jnp.exp(decay_col[:, 0:1]) * state
                          + lax.dot_general(k_dec, v, _TN, preferred_element_type=F32))
        o = o * lax.rsqrt(jnp.mean(o * o, axis=-1, keepdims=True) + RMS_EPS) * ng_ref[...]
        o_ref[rows, :] = (o * r_ref[rows, :]).astype(o_ref.dtype)


def _gla(q, k, v, a1, wa2_pad, ba, norm_g, r, batch, seq):
    t, kdim = q.shape
    vdim = v.shape[1]
    dv = norm_g.shape[-1]
    heads = vdim // dv
    dk = kdim // heads
    rows = _tile(seq, TILE["gla_rows"])
    ns = seq // rows

    def row_spec(width):
        return pl.BlockSpec((rows, width), lambda b, h, s: (b * ns + s, h))

    return pl.pallas_call(
        functools.partial(_gla_kernel, n_sub=rows // GLA_CHUNK),
        grid=(batch, heads, ns),
        in_specs=[
            row_spec(dk), row_spec(dk), row_spec(dv),
            pl.BlockSpec((rows, LANES), lambda b, h, s: (b * ns + s, 0)),
            pl.BlockSpec((LANES, dk), lambda b, h, s: (0, h)),
            pl.BlockSpec((1, dk), lambda b, h, s: (0, h)),
            pl.BlockSpec((1, dv), lambda b, h, s: (0, 0)),
            row_spec(dv),
        ],
        out_specs=row_spec(dv),
        out_shape=jax.ShapeDtypeStruct((t, vdim), BF16),
        scratch_shapes=[pltpu.VMEM((dk, dv), F32)],
        compiler_params=_params(("arbitrary", "arbitrary", "arbitrary"), 32),
        name="gla_chunk",
    )(q, k, v, a1, wa2_pad, ba, norm_g, r)


def _conv_mix_kernel(h_ref, wb_ref, wc_ref, wu_ref, cw_ref, o_ref, carry_ref, *, tiles_per_seq):
    @pl.when(pl.program_id(1) % tiles_per_seq == 0)
    def _():
        carry_ref[...] = jnp.zeros_like(carry_ref)

    h = h_ref[...]
    cu = _dot(h, wc_ref[...]) * _dot(h, wu_ref[...])
    y = _causal_conv3(cu, carry_ref[...], cw_ref)
    carry_ref[...] = cu[cu.shape[0] - SUBLANES:, :]
    o_ref[...] = (_dot(h, wb_ref[...]) * y).astype(o_ref.dtype)


def _conv_mix(h, w_in, conv_w, seq):
    t, d = h.shape
    n = w_in.shape[1] // 3
    tm, tn = _tile(seq, TILE["proj_m"]), _tile(n, TILE["proj_n"])
    nj = n // tn
    return pl.pallas_call(
        functools.partial(_conv_mix_kernel, tiles_per_seq=seq // tm),
        grid=(nj, t // tm),
        in_specs=[
            pl.BlockSpec((tm, d), lambda j, m: (m, 0)),
            pl.BlockSpec((d, tn), lambda j, m: (0, j)),
            pl.BlockSpec((d, tn), lambda j, m: (0, nj + j)),
            pl.BlockSpec((d, tn), lambda j, m: (0, 2 * nj + j)),
            pl.BlockSpec((conv_w.shape[0], tn), lambda j, m: (0, j)),
        ],
        out_specs=pl.BlockSpec((tm, tn), lambda j, m: (m, j)),
        out_shape=jax.ShapeDtypeStruct((t, n), BF16),
        scratch_shapes=[pltpu.VMEM((SUBLANES, tn), F32)],
        compiler_params=_params(("arbitrary", "arbitrary"), 48),
        name="conv_mix",
    )(h, w_in, w_in, w_in, conv_w)


def _ffn_up_kernel(h_ref, wa_ref, wu_ref, cwa_ref, cwu_ref, ba_ref, bu_ref, o_ref,
                   carry_a_ref, carry_u_ref, *, tiles_per_seq):
    @pl.when(pl.program_id(1) % tiles_per_seq == 0)
    def _():
        carry_a_ref[...] = jnp.zeros_like(carry_a_ref)
        carry_u_ref[...] = jnp.zeros_like(carry_u_ref)

    h = h_ref[...]
    za = _dot(h, wa_ref[...])
    zu = _dot(h, wu_ref[...])
    tm = za.shape[0]
    a = _causal_conv3(za, carry_a_ref[...], cwa_ref) + ba_ref[...]
    u = _causal_conv3(zu, carry_u_ref[...], cwu_ref) + bu_ref[...]
    carry_a_ref[...] = za[tm - SUBLANES:, :]
    carry_u_ref[...] = zu[tm - SUBLANES:, :]
    o_ref[...] = (_silu(a) * u).astype(o_ref.dtype)


def _ffn_up(h, w_up, conv_w, conv_b, seq):
    t, d = h.shape
    f = w_up.shape[1] // 2
    tm, tn = _tile(seq, TILE["proj_m"]), _tile(f, TILE["proj_n"])
    nj = f // tn
    cw = conv_w.shape[0]
    bias = conv_b.reshape(1, 2 * f)
    return pl.pallas_call(
        functools.partial(_ffn_up_kernel, tiles_per_seq=seq // tm),
        grid=(nj, t // tm),
        in_specs=[
            pl.BlockSpec((tm, d), lambda j, m: (m, 0)),
            pl.BlockSpec((d, tn), lambda j, m: (0, j)),
            pl.BlockSpec((d, tn), lambda j, m: (0, nj + j)),
            pl.BlockSpec((cw, tn), lambda j, m: (0, j)),
            pl.BlockSpec((cw, tn), lambda j, m: (0, nj + j)),
            pl.BlockSpec((1, tn), lambda j, m: (0, j)),
            pl.BlockSpec((1, tn), lambda j, m: (0, nj + j)),
        ],
        out_specs=pl.BlockSpec((tm, tn), lambda j, m: (m, j)),
        out_shape=jax.ShapeDtypeStruct((t, f), BF16),
        scratch_shapes=[pltpu.VMEM((SUBLANES, tn), F32), pltpu.VMEM((SUBLANES, tn), F32)],
        compiler_params=_params(("arbitrary", "arbitrary"), 48),
        name="ffn_up",
    )(h, w_up, w_up, conv_w, conv_w, bias, bias)


def _mm_ln_kernel(*refs, alpha, emit_h):
    a_ref, w_ref, x_ref, gate_ref, lng_ref, lnb_ref = refs[:6]
    if emit_h:
        sc_ref, sh_ref, xo_ref, h_ref = refs[6:10]
    else:
        xo_ref = refs[6]
    y_slots = refs[-2:]
    i = pl.program_id(0)
    last = pl.num_programs(0) - 1

    def project(slot):
        y_slots[slot][...] = _dot(a_ref[...], w_ref[...])

    def normalise(slot):
        z = alpha * x_ref[...] + gate_ref[0] * y_slots[slot][...]
        mu = jnp.mean(z, axis=-1, keepdims=True)
        zc = z - mu
        var = jnp.mean(zc * zc, axis=-1, keepdims=True)
        xn = zc * lax.rsqrt(var + LN_EPS) * lng_ref[...] + lnb_ref[...]
        xo_ref[...] = xn
        if emit_h:
            h_ref[...] = (xn * (1.0 + sc_ref[0]) + sh_ref[0]).astype(h_ref.dtype)

    @pl.when(i == 0)
    def _():
        project(0)

    for slot in (0, 1):
        mine = i % 2 == slot

        @pl.when(jnp.logical_and(mine, jnp.logical_and(i > 0, i < last)))
        def _():
            project(slot)
            normalise(1 - slot)

        @pl.when(jnp.logical_and(mine, i == last))
        def _():
            normalise(1 - slot)


def _mm_ln(a, w, x2, mod_tab, layer, gate_chunk, ln_g, ln_b, next_mod, alpha, seq, name):
    t, kdim = a.shape
    d = w.shape[1]
    tm = _tile(seq, TILE["ln_m"] if kdim <= TILE["ln_k_small"] else TILE["ln_m_big_k"])
    n = t // tm
    tiles_per_batch = seq // tm
    emit_h = next_mod is not None
    row = pl.BlockSpec((tm, d), lambda i: (jnp.maximum(i - 1, 0), 0))
    vec = pl.BlockSpec((1, d), lambda i: (0, 0))
    in_specs = [
        pl.BlockSpec((tm, kdim), lambda i: (jnp.minimum(i, n - 1), 0)),
        pl.BlockSpec((kdim, d), lambda i: (0, 0), pipeline_mode=pl.Buffered(1)),
        row,
        _mod_spec(d, layer, gate_chunk, tiles_per_batch, lag=1),
        vec, vec,
    ]
    args = [a, w, x2, mod_tab, ln_g.reshape(1, d), ln_b.reshape(1, d)]
    out_specs = [row]
    out_shape = [jax.ShapeDtypeStruct((t, d), F32)]
    if emit_h:
        nl, sc_chunk, sh_chunk = next_mod
        in_specs += [_mod_spec(d, nl, sc_chunk, tiles_per_batch, lag=1),
                     _mod_spec(d, nl, sh_chunk, tiles_per_batch, lag=1)]
        args += [mod_tab, mod_tab]
        out_specs.append(row)
        out_shape.append(jax.ShapeDtypeStruct((t, d), BF16))
    outs = pl.pallas_call(
        functools.partial(_mm_ln_kernel, alpha=alpha, emit_h=emit_h),
        grid=(n + 1,),
        in_specs=in_specs,
        out_specs=out_specs,
        out_shape=out_shape,
        scratch_shapes=[pltpu.VMEM((tm, d), F32), pltpu.VMEM((tm, d), F32)],
        compiler_params=_params(("arbitrary",), 56),
        name=name,
    )(*args)
    return (outs[0], outs[1]) if emit_h else (outs[0], None)


def _pad_cols(w, n):
    return jnp.pad(w, ((0, 0), (0, n - w.shape[1])))


def _fox_mixer(h, wq, wk, wv, wg, wf, bf, batch, seq):
    t, d = h.shape
    heads = wf.shape[1]
    scale = (d // heads) ** -0.5 * LOG2E
    q, k, g = _multi_proj(
        h, [wq.astype(BF16), wk.astype(BF16), wg.astype(BF16)],
        [lambda z: z * scale, lambda z: z, jax.nn.sigmoid],
        [BF16, BF16, F32], "fox_proj")
    vt = _proj_t(h, wv.T.astype(BF16), BF16, "fox_proj_vt")
    qb, kb = _fox_cum(h, _pad_cols(wf, LANES).astype(BF16),
                      _pad_cols(bf.reshape(1, heads), LANES), batch, seq, heads)
    shape3 = (batch, seq, d)
    o = _fox_attn(q.reshape(shape3), qb.reshape(shape3), k.reshape(shape3), kb.reshape(shape3),
                  vt, g.reshape(shape3), heads)
    return o.reshape(t, d)


def _gla_mixer(h, wq, wk, wv, wa1, wa2, ba, wr, norm_g, batch, seq):
    dv = norm_g.shape[-1]
    heads = wv.shape[1] // dv
    dk = wq.shape[1] // heads
    scale = dk ** -0.5
    q, k = _multi_proj(h, [wq.astype(BF16), wk.astype(BF16)],
                       [lambda z: z * scale, lambda z: z], [F32, F32], "gla_proj_qk")
    v, r = _multi_proj(h, [wv.astype(BF16), wr.astype(BF16)],
                       [lambda z: z, _silu], [BF16, F32], "gla_proj_vr")
    (a1,) = _multi_proj(h, [_pad_cols(wa1, LANES).astype(BF16)], [lambda z: z], [F32],
                        "gla_proj_a1")
    wa2_pad = jnp.pad(wa2, ((0, LANES - wa2.shape[0]), (0, 0))).astype(BF16)
    return _gla(q, k, v, a1, wa2_pad, ba.reshape(1, -1), norm_g.reshape(1, dv), r, batch, seq)


def kernel(x, c, ada_w, ada_b, ln1_g, ln1_b, ln2_g, ln2_b, fox_wq, fox_wk, fox_wv, fox_wg, fox_wf, fox_bf, fox_wo, gla_wq, gla_wk, gla_wv, gla_wa1, gla_wa2, gla_ba, gla_wr, gla_norm_g, gla_wo, conv_w_in, conv_w, conv_w_out, ffn_w_up, ffn_conv_w, ffn_conv_b, ffn_w_down):
    batch, seq, d = x.shape
    depth = ada_w.shape[0]
    assert batch <= MOD_ROWS
    alpha = (2 * depth) ** 0.25
    t = batch * seq

    c_pad = jnp.pad(c, ((0, MOD_ROWS - batch), (0, 0)))
    mod_tab = _ada_mod(c_pad, ada_w, ada_b).reshape(depth * MOD_ROWS * 6, 1, d)

    x2 = x.reshape(t, d)
    h = _modulate(x2, mod_tab, 0, seq)
    for i in range(depth):
        kind, j = i % 3, i // 3
        if kind == 0:
            y = _fox_mixer(h, fox_wq[j], fox_wk[j], fox_wv[j], fox_wg[j], fox_wf[j], fox_bf[j],
                           batch, seq)
            wo = fox_wo[j]
        elif kind == 1:
            y = _gla_mixer(h, gla_wq[j], gla_wk[j], gla_wv[j], gla_wa1[j], gla_wa2[j], gla_ba[j],
                           gla_wr[j], gla_norm_g[j], batch, seq)
            wo = gla_wo[j]
        else:
            y = _conv_mix(h, conv_w_in[j].astype(BF16), conv_w[j], seq)
            wo = conv_w_out[j]
        x2, h = _mm_ln(y, wo.astype(BF16), x2, mod_tab, i, 2, ln1_g[i], ln1_b[i], (i, 4, 3),
                       alpha, seq, "mixer_out_ln")
        act = _ffn_up(h, ffn_w_up[i].astype(BF16), ffn_conv_w[i], ffn_conv_b[i], seq)
        next_mod = (i + 1, 1, 0) if i + 1 < depth else None
        x2, h = _mm_ln(act, ffn_w_down[i].astype(BF16), x2, mod_tab, i, 5, ln2_g[i], ln2_b[i],
                       next_mod, alpha, seq, "ffn_down_ln")
    return x2.reshape(batch, seq, d)
```

```python
import functools

import numpy as np
import jax
import jax.numpy as jnp
from jax import lax
from jax.experimental import pallas as pl
from jax.experimental.pallas import tpu as pltpu

F32 = jnp.float32
BF16 = jnp.bfloat16

LANES = 128
SUBLANES = 8
MOD_ROWS = SUBLANES
MIB = 1 << 20

LOG2E = 1.4426950408889634
GLA_CHUNK = 64
GLA_GATE_TAU = 16.0
LN_EPS = 1e-5
RMS_EPS = 1e-5

TILE = dict(
    ada_n=1024,
    rows=512,
    proj_m=1024,
    proj_n=512,
    attn=512,
    gla_rows=16 * GLA_CHUNK,
    ln_m=512,
    ln_m_big_k=256,
    ln_k_small=2048,
)

_NT = (((1,), (1,)), ((), ()))
_TN = (((0,), (0,)), ((), ()))


def _params(semantics, vmem_mib):
    return pltpu.CompilerParams(dimension_semantics=semantics,
                                vmem_limit_bytes=vmem_mib * MIB)


def _tile(n, pref):
    t = min(n, pref)
    assert n % t == 0, (n, pref)
    return t


def _dot(a, b):
    return jnp.dot(a, b, preferred_element_type=F32)


def _log_sigmoid(z):
    return jnp.minimum(z, 0.0) - jnp.log1p(jnp.exp(-jnp.abs(z)))


def _silu(z):
    return z * jax.nn.sigmoid(z)


def _split3(x):
    hi = x.astype(BF16)
    r1 = x - hi.astype(F32)
    mid = r1.astype(BF16)
    lo = (r1 - mid.astype(F32)).astype(BF16)
    return hi, mid, lo


def _tril_mask(n):
    r = lax.broadcasted_iota(jnp.int32, (n, n), 0)
    c = lax.broadcasted_iota(jnp.int32, (n, n), 1)
    return c <= r


def _cumsum_rows(tril_bf16, x):
    hi, mid, lo = _split3(x)
    return (_dot(tril_bf16, lo) + _dot(tril_bf16, mid)) + _dot(tril_bf16, hi)


def _causal_conv3(x, prev, w_ref):
    rows = lax.broadcasted_iota(jnp.int32, x.shape, 0)
    p6 = prev[SUBLANES - 2:SUBLANES - 1, :]
    p7 = prev[SUBLANES - 1:SUBLANES, :]
    x1 = jnp.where(rows == 0, p7, pltpu.roll(x, 1, 0))
    x2 = jnp.where(rows == 0, p6, jnp.where(rows == 1, p7, pltpu.roll(x, 2, 0)))
    return x2 * w_ref[0:1, :] + x1 * w_ref[1:2, :] + x * w_ref[2:3, :]


def _ada_kernel(c_ref, w_ref, b_ref, o_ref):
    cond = _silu(c_ref[...]).astype(BF16)
    o_ref[0] = _dot(cond, w_ref[0].astype(BF16)) + b_ref[0]


def _ada_mod(c_pad, ada_w, ada_b):
    depth, d, n = ada_w.shape
    tn = _tile(n, TILE["ada_n"])
    return pl.pallas_call(
        _ada_kernel,
        grid=(depth, n // tn),
        in_specs=[
            pl.BlockSpec((MOD_ROWS, d), lambda i, j: (0, 0)),
            pl.BlockSpec((1, d, tn), lambda i, j: (i, 0, j)),
            pl.BlockSpec((1, 1, tn), lambda i, j: (i, 0, j)),
        ],
        out_specs=pl.BlockSpec((1, MOD_ROWS, tn), lambda i, j: (i, 0, j)),
        out_shape=jax.ShapeDtypeStruct((depth, MOD_ROWS, n), F32),
        compiler_params=_params(("arbitrary", "arbitrary"), 40),
        name="ada_mod",
    )(c_pad, ada_w, ada_b.reshape(depth, 1, n))


def _mod_spec(d, layer, chunk, tiles_per_batch, lag=0):
    def index(m, *_):
        tile = jnp.maximum(m - lag, 0) if lag else m
        return ((layer * MOD_ROWS + tile // tiles_per_batch) * 6 + chunk, 0, 0)
    return pl.BlockSpec((1, 1, d), index)


def _modulate_kernel(x_ref, sc_ref, sh_ref, h_ref):
    h_ref[...] = (x_ref[...] * (1.0 + sc_ref[0]) + sh_ref[0]).astype(BF16)


def _modulate(x2, mod_tab, layer, seq):
    t, d = x2.shape
    tm = _tile(seq, TILE["rows"])
    return pl.pallas_call(
        _modulate_kernel,
        grid=(t // tm,),
        in_specs=[
            pl.BlockSpec((tm, d), lambda m: (m, 0)),
            _mod_spec(d, layer, 1, seq // tm),
            _mod_spec(d, layer, 0, seq // tm),
        ],
        out_specs=pl.BlockSpec((tm, d), lambda m: (m, 0)),
        out_shape=jax.ShapeDtypeStruct((t, d), BF16),
        compiler_params=_params(("arbitrary",), 32),
        name="modulate",
    )(x2, mod_tab, mod_tab)


def _cast_weights(first, w_refs, w16_refs):
    @pl.when(first)
    def _():
        for w_ref, w16_ref in zip(w_refs, w16_refs):
            w16_ref[...] = w_ref[...].astype(BF16)


def _multi_proj_kernel(h_ref, *refs, posts):
    n = len(posts)
    w_refs, o_refs, w16_refs = refs[:n], refs[n:2 * n], refs[2 * n:]
    _cast_weights(pl.program_id(1) == 0, w_refs, w16_refs)
    h = h_ref[...]
    for w16_ref, o_ref, post in zip(w16_refs, o_refs, posts):
        o_ref[...] = post(_dot(h, w16_ref[...])).astype(o_ref.dtype)


def _multi_proj(h, weights, posts, out_dtypes, name):
    t, d = h.shape
    n = weights[0].shape[1]
    tm, tn = _tile(t, TILE["proj_m"]), _tile(n, TILE["proj_n"])
    w_spec = pl.BlockSpec((d, tn), lambda j, m: (0, j), pipeline_mode=pl.Buffered(1))
    o_spec = pl.BlockSpec((tm, tn), lambda j, m: (m, j))
    return pl.pallas_call(
        functools.partial(_multi_proj_kernel, posts=tuple(posts)),
        grid=(n // tn, t // tm),
        in_specs=[pl.BlockSpec((tm, d), lambda j, m: (m, 0))] + [w_spec] * len(weights),
        out_specs=[o_spec] * len(weights),
        out_shape=[jax.ShapeDtypeStruct((t, n), dt) for dt in out_dtypes],
        scratch_shapes=[pltpu.VMEM((d, tn), BF16)] * len(weights),
        compiler_params=_params(("arbitrary", "arbitrary"), 48),
        name=name,
    )(h, *weights)


def _proj_t_kernel(w_ref, h_ref, o_ref, w16_ref):
    _cast_weights(pl.program_id(1) == 0, [w_ref], [w16_ref])
    o_ref[...] = lax.dot_general(w16_ref[...], h_ref[...], _NT,
                                 preferred_element_type=F32).astype(o_ref.dtype)


def _proj_t(h, w_t, out_dtype, name):
    t, d = h.shape
    n = w_t.shape[0]
    tm, tn = _tile(t, TILE["proj_m"]), _tile(n, TILE["proj_n"])
    return pl.pallas_call(
        _proj_t_kernel,
        grid=(n // tn, t // tm),
        in_specs=[pl.BlockSpec((tn, d), lambda j, m: (j, 0), pipeline_mode=pl.Buffered(1)),
                  pl.BlockSpec((tm, d), lambda j, m: (m, 0))],
        out_specs=pl.BlockSpec((tn, tm), lambda j, m: (j, m)),
        out_shape=jax.ShapeDtypeStruct((n, t), out_dtype),
        scratch_shapes=[pltpu.VMEM((tn, d), BF16)],
        compiler_params=_params(("arbitrary", "arbitrary"), 32),
        name=name,
    )(w_t, h)


def _fox_cum_kernel(h_ref, wf_ref, bf_ref, pq_ref, pk_ref, cq_ref, ck_ref, qb_ref, kb_ref,
                    carry_ref):
    @pl.when(pl.program_id(1) == 0)
    def _():
        carry_ref[...] = jnp.zeros_like(carry_ref)

    log_f = _log_sigmoid(_dot(h_ref[...], wf_ref[...]) + bf_ref[...])
    ts = log_f.shape[0]
    tril = jnp.where(_tril_mask(ts), 1.0, 0.0).astype(BF16)
    cum = _cumsum_rows(tril, log_f) + carry_ref[...]
    carry_ref[...] = cum[ts - 1:ts, :]
    parts = jnp.concatenate(_split3(cum * LOG2E), axis=1)
    qb_ref[...] = (_dot(parts, pq_ref[...]) + cq_ref[...]).astype(qb_ref.dtype)
    kb_ref[...] = (_dot(parts, pk_ref[...]) + ck_ref[...]).astype(kb_ref.dtype)


def _bias_placement(heads):
    pq = np.zeros((3 * LANES, heads * LANES), np.float32)
    pk = np.zeros((3 * LANES, heads * LANES), np.float32)
    cq = np.zeros((1, heads * LANES), np.float32)
    ck = np.zeros((1, heads * LANES), np.float32)
    for hd in range(heads):
        for term in range(3):
            pq[term * LANES + hd, hd * LANES + term] = 1.0
            pk[term * LANES + hd, hd * LANES + 3 + term] = -1.0
        cq[0, hd * LANES + 3:hd * LANES + 6] = 1.0
        ck[0, hd * LANES:hd * LANES + 3] = 1.0
    return jnp.asarray(pq, BF16), jnp.asarray(pk, BF16), jnp.asarray(cq), jnp.asarray(ck)


def _fox_cum(h, wf_pad, bf_pad, batch, seq, heads):
    t, d = h.shape
    ts = _tile(seq, TILE["rows"])
    ns = seq // ts
    width = heads * LANES

    def const(b, s):
        return (0, 0)

    out_spec = pl.BlockSpec((ts, width), lambda b, s: (b * ns + s, 0))
    return pl.pallas_call(
        _fox_cum_kernel,
        grid=(batch, ns),
        in_specs=[
            pl.BlockSpec((ts, d), lambda b, s: (b * ns + s, 0)),
            pl.BlockSpec((d, LANES), const),
            pl.BlockSpec((1, LANES), const),
            pl.BlockSpec((3 * LANES, width), const),
            pl.BlockSpec((3 * LANES, width), const),
            pl.BlockSpec((1, width), const),
            pl.BlockSpec((1, width), const),
        ],
        out_specs=[out_spec, out_spec],
        out_shape=[jax.ShapeDtypeStruct((t, width), BF16)] * 2,
        scratch_shapes=[pltpu.VMEM((1, LANES), F32)],
        compiler_params=_params(("arbitrary", "arbitrary"), 32),
        name="fox_cum",
    )(h, wf_pad, bf_pad, *_bias_placement(heads))


def _fox_attn_kernel(q_ref, qb_ref, k_ref, kb_ref, vt_ref, g_ref, o_ref,
                     m_ref, acc_ref, l0_ref, l1_ref, s0_ref, s1_ref, smax0_ref, smax1_ref,
                     p0_ref, p1_ref, alpha0_ref, alpha1_ref, *, tq):
    nq = q_ref.shape[0] // tq
    pairs = [(qi, j) for qi in range(nq) for j in range(qi + 1)]
    s_slots, smax_slots = (s0_ref, s1_ref), (smax0_ref, smax1_ref)
    p_slots, alpha_slots = (p0_ref, p1_ref), (alpha0_ref, alpha1_ref)
    l_slots = (l0_ref, l1_ref)

    def rows(i):
        return slice(i * tq, (i + 1) * tq)

    def scores(t):
        qi, j = pairs[t]
        q = jnp.concatenate([q_ref[rows(qi), :], qb_ref[rows(qi), :]], axis=1)
        k = jnp.concatenate([k_ref[rows(j), :], kb_ref[rows(j), :]], axis=1)
        s = lax.dot_general(k, q, _NT, preferred_element_type=F32)
        if j == qi:
            key = lax.broadcasted_iota(jnp.int32, (tq, tq), 0)
            query = lax.broadcasted_iota(jnp.int32, (tq, tq), 1)
            s = jnp.where(key <= query, s, -jnp.inf)
        s_slots[t % 2][...] = s
        smax_slots[t % 2][...] = jnp.max(s, axis=0, keepdims=True)

    def numerators(t):
        qi, j = pairs[t]
        l_ref = l_slots[qi % 2]
        if j == 0:
            m_new = smax_slots[t % 2][...]
            p = jnp.exp2(s_slots[t % 2][...] - m_new)
            l_ref[...] = jnp.sum(p, axis=0, keepdims=True)
        else:
            m_prev = m_ref[...]
            m_new = jnp.maximum(m_prev, smax_slots[t % 2][...])
            alpha = jnp.exp2(m_prev - m_new)
            alpha_slots[t % 2][...] = alpha
            p = jnp.exp2(s_slots[t % 2][...] - m_new)
            l_ref[...] = alpha * l_ref[...] + jnp.sum(p, axis=0, keepdims=True)
        p_slots[t % 2][...] = p.astype(BF16)
        m_ref[...] = m_new

    def values(t):
        qi, j = pairs[t]
        acc = _dot(vt_ref[:, rows(j)], p_slots[t % 2][...])
        if j > 0:
            acc = alpha_slots[t % 2][...] * acc_ref[...] + acc
        if j < qi:
            acc_ref[...] = acc
        else:
            o = (acc / l_slots[qi % 2][...]).T
            o_ref[rows(qi), :] = (o * g_ref[rows(qi), :]).astype(o_ref.dtype)

    for step in range(len(pairs) + 2):
        if step >= 2:
            values(step - 2)
        if step < len(pairs):
            scores(step)
        if 1 <= step <= len(pairs):
            numerators(step - 1)


def _fox_attn(q, qb, k, kb, vt, g, heads):
    batch, seq, d = q.shape
    dh = d // heads
    assert dh == LANES and heads <= LANES
    tq = _tile(seq, TILE["attn"])
    head_spec = pl.BlockSpec((None, seq, dh), lambda b, h: (b, 0, h))
    stat = pltpu.VMEM((1, tq), F32)
    return pl.pallas_call(
        functools.partial(_fox_attn_kernel, tq=tq),
        grid=(batch, heads),
        in_specs=[
            head_spec, head_spec, head_spec, head_spec,
            pl.BlockSpec((dh, seq), lambda b, h: (h, b)),
            head_spec,
        ],
        out_specs=head_spec,
        out_shape=jax.ShapeDtypeStruct((batch, seq, d), BF16),
        scratch_shapes=[
            stat,
            pltpu.VMEM((dh, tq), F32),
            stat, stat,
            pltpu.VMEM((tq, tq), F32), pltpu.VMEM((tq, tq), F32),
            stat, stat,
            pltpu.VMEM((tq, tq), BF16), pltpu.VMEM((tq, tq), BF16),
            stat, stat,
        ],
        compiler_params=_params(("arbitrary", "arbitrary"), 40),
        name="fox_attn",
    )(q, qb, k, kb, vt, g)


def _gla_kernel(q_ref, k_ref, v_ref, a1_ref, wa2_ref, ba_ref, ng_ref, r_ref, o_ref,
                state_ref, *, n_sub):
    @pl.when(pl.program_id(2) == 0)
    def _():
        state_ref[...] = jnp.zeros_like(state_ref)

    c = GLA_CHUNK
    causal = _tril_mask(c)
    tril = jnp.where(causal, 1.0, 0.0).astype(BF16)
    ones = jnp.ones((c, LANES), BF16)
    for sub in range(n_sub):
        rows = slice(sub * c, (sub + 1) * c)
        gate = _dot(a1_ref[rows, :].astype(BF16), wa2_ref[...]) + ba_ref[...]
        log_a = _log_sigmoid(gate) / GLA_GATE_TAU
        la_hi, la_mid, la_lo = _split3(log_a)
        cb = (_dot(tril, la_lo) + _dot(tril, la_mid)) + _dot(tril, la_hi)
        cb_last = cb[c - 1:c, :]
        decay_col = ((lax.dot_general(la_lo, ones, _TN, preferred_element_type=F32)
                      + lax.dot_general(la_mid, ones, _TN, preferred_element_type=F32))
                     + lax.dot_general(la_hi, ones, _TN, preferred_element_type=F32))
        q = q_ref[rows, :]
        k = k_ref[rows, :]
        v = v_ref[rows, :]
        q_dec = (q * jnp.exp(cb)).astype(BF16)
        k_inv = (k * jnp.exp(-cb)).astype(BF16)
        k_dec = (k * jnp.exp(cb_last - cb)).astype(BF16)
        att = lax.dot_general(q_dec, k_inv, _NT, preferred_element_type=F32)
        att = jnp.where(causal, att, 0.0)
        state = state_ref[...]
        o = _dot(att.astype(BF16), v) + _dot(q_dec, state.astype(BF16))
        state_ref[...] = (jnp.exp(decay_col[:, 0:1]) * state
                          + lax.dot_general(k_dec, v, _TN, preferred_element_type=F32))
        o = o * lax.rsqrt(jnp.mean(o * o, axis=-1, keepdims=True) + RMS_EPS) * ng_ref[...]
        o_ref[rows, :] = (o * r_ref[rows, :]).astype(o_ref.dtype)


def _gla(q, k, v, a1, wa2_pad, ba, norm_g, r, batch, seq):
    t, kdim = q.shape
    vdim = v.shape[1]
    dv = norm_g.shape[-1]
    heads = vdim // dv
    dk = kdim // heads
    rows = _tile(seq, TILE["gla_rows"])
    ns = seq // rows

    def row_spec(width):
        return pl.BlockSpec((rows, width), lambda b, h, s: (b * ns + s, h))

    return pl.pallas_call(
        functools.partial(_gla_kernel, n_sub=rows // GLA_CHUNK),
        grid=(batch, heads, ns),
        in_specs=[
            row_spec(dk), row_spec(dk), row_spec(dv),
            pl.BlockSpec((rows, LANES), lambda b, h, s: (b * ns + s, 0)),
            pl.BlockSpec((LANES, dk), lambda b, h, s: (0, h)),
            pl.BlockSpec((1, dk), lambda b, h, s: (0, h)),
            pl.BlockSpec((1, dv), lambda b, h, s: (0, 0)),
            row_spec(dv),
        ],
        out_specs=row_spec(dv),
        out_shape=jax.ShapeDtypeStruct((t, vdim), BF16),
        scratch_shapes=[pltpu.VMEM((dk, dv), F32)],
        compiler_params=_params(("arbitrary", "arbitrary", "arbitrary"), 32),
        name="gla_chunk",
    )(q, k, v, a1, wa2_pad, ba, norm_g, r)


def _conv_mix_kernel(h_ref, wb_ref, wc_ref, wu_ref, cw_ref, o_ref,
                     wb16_ref, wc16_ref, wu16_ref, carry_ref, *, tiles_per_seq):
    _cast_weights(pl.program_id(1) == 0, [wb_ref, wc_ref, wu_ref],
                  [wb16_ref, wc16_ref, wu16_ref])

    @pl.when(pl.program_id(1) % tiles_per_seq == 0)
    def _():
        carry_ref[...] = jnp.zeros_like(carry_ref)

    h = h_ref[...]
    cu = _dot(h, wc16_ref[...]) * _dot(h, wu16_ref[...])
    y = _causal_conv3(cu, carry_ref[...], cw_ref)
    carry_ref[...] = cu[cu.shape[0] - SUBLANES:, :]
    o_ref[...] = (_dot(h, wb16_ref[...]) * y).astype(o_ref.dtype)


def _conv_mix(h, w_in, conv_w, seq):
    t, d = h.shape
    n = w_in.shape[1] // 3
    tm, tn = _tile(seq, TILE["proj_m"]), _tile(n, TILE["proj_n"])
    nj = n // tn
    return pl.pallas_call(
        functools.partial(_conv_mix_kernel, tiles_per_seq=seq // tm),
        grid=(nj, t // tm),
        in_specs=[
            pl.BlockSpec((tm, d), lambda j, m: (m, 0)),
            pl.BlockSpec((d, tn), lambda j, m: (0, j), pipeline_mode=pl.Buffered(1)),
            pl.BlockSpec((d, tn), lambda j, m: (0, nj + j), pipeline_mode=pl.Buffered(1)),
            pl.BlockSpec((d, tn), lambda j, m: (0, 2 * nj + j), pipeline_mode=pl.Buffered(1)),
            pl.BlockSpec((conv_w.shape[0], tn), lambda j, m: (0, j)),
        ],
        out_specs=pl.BlockSpec((tm, tn), lambda j, m: (m, j)),
        out_shape=jax.ShapeDtypeStruct((t, n), BF16),
        scratch_shapes=[pltpu.VMEM((d, tn), BF16)] * 3 + [pltpu.VMEM((SUBLANES, tn), F32)],
        compiler_params=_params(("arbitrary", "arbitrary"), 48),
        name="conv_mix",
    )(h, w_in, w_in, w_in, conv_w)


def _ffn_up_kernel(h_ref, wa_ref, wu_ref, cwa_ref, cwu_ref, ba_ref, bu_ref, o_ref,
                   wa16_ref, wu16_ref, carry_a_ref, carry_u_ref, *, tiles_per_seq):
    _cast_weights(pl.program_id(1) == 0, [wa_ref, wu_ref], [wa16_ref, wu16_ref])

    @pl.when(pl.program_id(1) % tiles_per_seq == 0)
    def _():
        carry_a_ref[...] = jnp.zeros_like(carry_a_ref)
        carry_u_ref[...] = jnp.zeros_like(carry_u_ref)

    h = h_ref[...]
    za = _dot(h, wa16_ref[...])
    zu = _dot(h, wu16_ref[...])
    tm = za.shape[0]
    a = _causal_conv3(za, carry_a_ref[...], cwa_ref) + ba_ref[...]
    u = _causal_conv3(zu, carry_u_ref[...], cwu_ref) + bu_ref[...]
    carry_a_ref[...] = za[tm - SUBLANES:, :]
    carry_u_ref[...] = zu[tm - SUBLANES:, :]
    o_ref[...] = (_silu(a) * u).astype(o_ref.dtype)


def _ffn_up(h, w_up, conv_w, conv_b, seq):
    t, d = h.shape
    f = w_up.shape[1] // 2
    tm, tn = _tile(seq, TILE["proj_m"]), _tile(f, TILE["proj_n"])
    nj = f // tn
    cw = conv_w.shape[0]
    bias = conv_b.reshape(1, 2 * f)
    halo = pltpu.VMEM((SUBLANES, tn), F32)
    return pl.pallas_call(
        functools.partial(_ffn_up_kernel, tiles_per_seq=seq // tm),
        grid=(nj, t // tm),
        in_specs=[
            pl.BlockSpec((tm, d), lambda j, m: (m, 0)),
            pl.BlockSpec((d, tn), lambda j, m: (0, j)),
            pl.BlockSpec((d, tn), lambda j, m: (0, nj + j)),
            pl.BlockSpec((cw, tn), lambda j, m: (0, j)),
            pl.BlockSpec((cw, tn), lambda j, m: (0, nj + j)),
            pl.BlockSpec((1, tn), lambda j, m: (0, j)),
            pl.BlockSpec((1, tn), lambda j, m: (0, nj + j)),
        ],
        out_specs=pl.BlockSpec((tm, tn), lambda j, m: (m, j)),
        out_shape=jax.ShapeDtypeStruct((t, f), BF16),
        scratch_shapes=[pltpu.VMEM((d, tn), BF16)] * 2 + [halo] * 2,
        compiler_params=_params(("arbitrary", "arbitrary"), 56),
        name="ffn_up",
    )(h, w_up, w_up, conv_w, conv_w, bias, bias)


def _mm_ln_kernel(*refs, alpha, emit_h):
    a_ref, w_ref, x_ref, gate_ref, lng_ref, lnb_ref = refs[:6]
    if emit_h:
        sc_ref, sh_ref, xo_ref, h_ref = refs[6:10]
    else:
        xo_ref = refs[6]
    y_slots = refs[-2:]
    i = pl.program_id(0)
    last = pl.num_programs(0) - 1

    def project(slot):
        y_slots[slot][...] = _dot(a_ref[...], w_ref[...])

    def normalise(slot):
        z = alpha * x_ref[...] + gate_ref[0] * y_slots[slot][...]
        mu = jnp.mean(z, axis=-1, keepdims=True)
        zc = z - mu
        var = jnp.mean(zc * zc, axis=-1, keepdims=True)
        xn = zc * lax.rsqrt(var + LN_EPS) * lng_ref[...] + lnb_ref[...]
        xo_ref[...] = xn
        if emit_h:
            h_ref[...] = (xn * (1.0 + sc_ref[0]) + sh_ref[0]).astype(h_ref.dtype)

    @pl.when(i == 0)
    def _():
        project(0)

    for slot in (0, 1):
        mine = i % 2 == slot

        @pl.when(jnp.logical_and(mine, jnp.logical_and(i > 0, i < last)))
        def _():
            project(slot)
            normalise(1 - slot)

        @pl.when(jnp.logical_and(mine, i == last))
        def _():
            normalise(1 - slot)


def _mm_ln(a, w, x2, mod_tab, layer, gate_chunk, ln_g, ln_b, next_mod, alpha, seq, name):
    t, kdim = a.shape
    d = w.shape[1]
    tm = _tile(seq, TILE["ln_m"] if kdim <= TILE["ln_k_small"] else TILE["ln_m_big_k"])
    n = t // tm
    tiles_per_batch = seq // tm
    emit_h = next_mod is not None
    row = pl.BlockSpec((tm, d), lambda i: (jnp.maximum(i - 1, 0), 0))
    vec = pl.BlockSpec((1, d), lambda i: (0, 0))
    in_specs = [
        pl.BlockSpec((tm, kdim), lambda i: (jnp.minimum(i, n - 1), 0)),
        pl.BlockSpec((kdim, d), lambda i: (0, 0), pipeline_mode=pl.Buffered(1)),
        row,
        _mod_spec(d, layer, gate_chunk, tiles_per_batch, lag=1),
        vec, vec,
    ]
    args = [a, w, x2, mod_tab, ln_g.reshape(1, d), ln_b.reshape(1, d)]
    out_specs = [row]
    out_shape = [jax.ShapeDtypeStruct((t, d), F32)]
    if emit_h:
        nl, sc_chunk, sh_chunk = next_mod
        in_specs += [_mod_spec(d, nl, sc_chunk, tiles_per_batch, lag=1),
                     _mod_spec(d, nl, sh_chunk, tiles_per_batch, lag=1)]
        args += [mod_tab, mod_tab]
        out_specs.append(row)
        out_shape.append(jax.ShapeDtypeStruct((t, d), BF16))
    outs = pl.pallas_call(
        functools.partial(_mm_ln_kernel, alpha=alpha, emit_h=emit_h),
        grid=(n + 1,),
        in_specs=in_specs,
        out_specs=out_specs,
        out_shape=out_shape,
        scratch_shapes=[pltpu.VMEM((tm, d), F32), pltpu.VMEM((tm, d), F32)],
        compiler_params=_params(("arbitrary",), 56),
        name=name,
    )(*args)
    return (outs[0], outs[1]) if emit_h else (outs[0], None)


def _pad_cols(w, n):
    return jnp.pad(w, ((0, 0), (0, n - w.shape[1])))


def _fox_mixer(h, wq, wk, wv, wg, wf, bf, batch, seq):
    t, d = h.shape
    heads = wf.shape[1]
    scale = (d // heads) ** -0.5 * LOG2E
    q, k, g = _multi_proj(
        h, [wq, wk, wg],
        [lambda z: z * scale, lambda z: z, jax.nn.sigmoid],
        [BF16, BF16, F32], "fox_proj")
    vt = _proj_t(h, wv.T, BF16, "fox_proj_vt")
    qb, kb = _fox_cum(h, _pad_cols(wf, LANES).astype(BF16),
                      _pad_cols(bf.reshape(1, heads), LANES), batch, seq, heads)
    shape3 = (batch, seq, d)
    o = _fox_attn(q.reshape(shape3), qb.reshape(shape3), k.reshape(shape3), kb.reshape(shape3),
                  vt, g.reshape(shape3), heads)
    return o.reshape(t, d)


def _gla_mixer(h, wq, wk, wv, wa1, wa2, ba, wr, norm_g, batch, seq):
    dv = norm_g.shape[-1]
    heads = wv.shape[1] // dv
    dk = wq.shape[1] // heads
    scale = dk ** -0.5
    q, k = _multi_proj(h, [wq, wk],
                       [lambda z: z * scale, lambda z: z], [F32, F32], "gla_proj_qk")
    v, r = _multi_proj(h, [wv, wr],
                       [lambda z: z, _silu], [BF16, F32], "gla_proj_vr")
    (a1,) = _multi_proj(h, [_pad_cols(wa1, LANES)], [lambda z: z], [F32],
                        "gla_proj_a1")
    wa2_pad = jnp.pad(wa2, ((0, LANES - wa2.shape[0]), (0, 0))).astype(BF16)
    return _gla(q, k, v, a1, wa2_pad, ba.reshape(1, -1), norm_g.reshape(1, dv), r, batch, seq)


def kernel(x, c, ada_w, ada_b, ln1_g, ln1_b, ln2_g, ln2_b, fox_wq, fox_wk, fox_wv, fox_wg, fox_wf, fox_bf, fox_wo, gla_wq, gla_wk, gla_wv, gla_wa1, gla_wa2, gla_ba, gla_wr, gla_norm_g, gla_wo, conv_w_in, conv_w, conv_w_out, ffn_w_up, ffn_conv_w, ffn_conv_b, ffn_w_down):
    batch, seq, d = x.shape
    depth = ada_w.shape[0]
    assert batch <= MOD_ROWS
    alpha = (2 * depth) ** 0.25
    t = batch * seq

    c_pad = jnp.pad(c, ((0, MOD_ROWS - batch), (0, 0)))
    mod_tab = _ada_mod(c_pad, ada_w, ada_b).reshape(depth * MOD_ROWS * 6, 1, d)

    x2 = x.reshape(t, d)
    h = _modulate(x2, mod_tab, 0, seq)
    for i in range(depth):
        kind, j = i % 3, i // 3
        if kind == 0:
            y = _fox_mixer(h, fox_wq[j], fox_wk[j], fox_wv[j], fox_wg[j], fox_wf[j], fox_bf[j],
                           batch, seq)
            wo = fox_wo[j]
        elif kind == 1:
            y = _gla_mixer(h, gla_wq[j], gla_wk[j], gla_wv[j], gla_wa1[j], gla_wa2[j], gla_ba[j],
                           gla_wr[j], gla_norm_g[j], batch, seq)
            wo = gla_wo[j]
        else:
            y = _conv_mix(h, conv_w_in[j], conv_w[j], seq)
            wo = conv_w_out[j]
        x2, h = _mm_ln(y, wo.astype(BF16), x2, mod_tab, i, 2, ln1_g[i], ln1_b[i], (i, 4, 3),
                       alpha, seq, "mixer_out_ln")
        act = _ffn_up(h, ffn_w_up[i], ffn_conv_w[i], ffn_conv_b[i], seq)
        next_mod = (i + 1, 1, 0) if i + 1 < depth else None
        x2, h = _mm_ln(act, ffn_w_down[i].astype(BF16), x2, mod_tab, i, 5, ln2_g[i], ln2_b[i],
                       next_mod, alpha, seq, "ffn_down_ln")
    return x2.reshape(batch, seq, d)
```

```python
import functools

import numpy as np
import jax
import jax.numpy as jnp
from jax import lax
from jax.experimental import pallas as pl
from jax.experimental.pallas import tpu as pltpu

F32 = jnp.float32
BF16 = jnp.bfloat16

LANES = 128
SUBLANES = 8
MOD_ROWS = SUBLANES
MIB = 1 << 20

LOG2E = 1.4426950408889634
GLA_CHUNK = 64
GLA_GATE_TAU = 16.0
LN_EPS = 1e-5
RMS_EPS = 1e-5

TILE = dict(
    ada_n=1024,
    rows=512,
    proj_m=1024,
    proj_n=512,
    proj_t_m=512,
    proj_t_n=1024,
    attn=512,
    gla_rows=16 * GLA_CHUNK,
    ln_m=512,
    ln_m_big_k=256,
    ln_k_small=2048,
)

_NT = (((1,), (1,)), ((), ()))
_TN = (((0,), (0,)), ((), ()))


def _params(semantics, vmem_mib):
    return pltpu.CompilerParams(dimension_semantics=semantics,
                                vmem_limit_bytes=vmem_mib * MIB)


def _tile(n, pref):
    t = min(n, pref)
    assert n % t == 0, (n, pref)
    return t


def _dot(a, b):
    return jnp.dot(a, b, preferred_element_type=F32)


def _log_sigmoid(z):
    return jnp.minimum(z, 0.0) - jnp.log1p(jnp.exp(-jnp.abs(z)))


def _silu(z):
    return z * jax.nn.sigmoid(z)


def _split3(x):
    hi = x.astype(BF16)
    r1 = x - hi.astype(F32)
    mid = r1.astype(BF16)
    lo = (r1 - mid.astype(F32)).astype(BF16)
    return hi, mid, lo


def _tril_mask(n):
    r = lax.broadcasted_iota(jnp.int32, (n, n), 0)
    c = lax.broadcasted_iota(jnp.int32, (n, n), 1)
    return c <= r


def _cumsum_rows(tril_bf16, x):
    hi, mid, lo = _split3(x)
    return (_dot(tril_bf16, lo) + _dot(tril_bf16, mid)) + _dot(tril_bf16, hi)


def _causal_conv3(x, prev, w_ref):
    rows = lax.broadcasted_iota(jnp.int32, x.shape, 0)
    p6 = prev[SUBLANES - 2:SUBLANES - 1, :]
    p7 = prev[SUBLANES - 1:SUBLANES, :]
    x1 = jnp.where(rows == 0, p7, pltpu.roll(x, 1, 0))
    x2 = jnp.where(rows == 0, p6, jnp.where(rows == 1, p7, pltpu.roll(x, 2, 0)))
    return x2 * w_ref[0:1, :] + x1 * w_ref[1:2, :] + x * w_ref[2:3, :]


def _ada_kernel(c_ref, w_ref, b_ref, o_ref):
    cond = _silu(c_ref[...]).astype(BF16)
    o_ref[0] = _dot(cond, w_ref[0].astype(BF16)) + b_ref[0]


def _ada_mod(c_pad, ada_w, ada_b):
    depth, d, n = ada_w.shape
    tn = _tile(n, TILE["ada_n"])
    return pl.pallas_call(
        _ada_kernel,
        grid=(depth, n // tn),
        in_specs=[
            pl.BlockSpec((MOD_ROWS, d), lambda i, j: (0, 0)),
            pl.BlockSpec((1, d, tn), lambda i, j: (i, 0, j)),
            pl.BlockSpec((1, 1, tn), lambda i, j: (i, 0, j)),
        ],
        out_specs=pl.BlockSpec((1, MOD_ROWS, tn), lambda i, j: (i, 0, j)),
        out_shape=jax.ShapeDtypeStruct((depth, MOD_ROWS, n), F32),
        compiler_params=_params(("arbitrary", "arbitrary"), 40),
        name="ada_mod",
    )(c_pad, ada_w, ada_b.reshape(depth, 1, n))


def _mod_spec(d, layer, chunk, tiles_per_batch, lag=0):
    def index(m, *_):
        tile = jnp.maximum(m - lag, 0) if lag else m
        return ((layer * MOD_ROWS + tile // tiles_per_batch) * 6 + chunk, 0, 0)
    return pl.BlockSpec((1, 1, d), index)


def _modulate_kernel(x_ref, sc_ref, sh_ref, h_ref):
    h_ref[...] = (x_ref[...] * (1.0 + sc_ref[0]) + sh_ref[0]).astype(BF16)


def _modulate(x2, mod_tab, layer, seq):
    t, d = x2.shape
    tm = _tile(seq, TILE["rows"])
    return pl.pallas_call(
        _modulate_kernel,
        grid=(t // tm,),
        in_specs=[
            pl.BlockSpec((tm, d), lambda m: (m, 0)),
            _mod_spec(d, layer, 1, seq // tm),
            _mod_spec(d, layer, 0, seq // tm),
        ],
        out_specs=pl.BlockSpec((tm, d), lambda m: (m, 0)),
        out_shape=jax.ShapeDtypeStruct((t, d), BF16),
        compiler_params=_params(("arbitrary",), 32),
        name="modulate",
    )(x2, mod_tab, mod_tab)


def _cast_weights(first, w_refs, w16_refs):
    @pl.when(first)
    def _():
        for w_ref, w16_ref in zip(w_refs, w16_refs):
            w16_ref[...] = w_ref[...].astype(BF16)


def _multi_proj_kernel(h_ref, *refs, posts):
    n = len(posts)
    w_refs, o_refs, w16_refs = refs[:n], refs[n:2 * n], refs[2 * n:]
    _cast_weights(pl.program_id(1) == 0, w_refs, w16_refs)
    h = h_ref[...]
    for w16_ref, o_ref, post in zip(w16_refs, o_refs, posts):
        o_ref[...] = post(_dot(h, w16_ref[...])).astype(o_ref.dtype)


def _multi_proj(h, weights, layer, posts, out_dtypes, name):
    t, d = h.shape
    n = weights[0].shape[2]
    tm, tn = _tile(t, TILE["proj_m"]), _tile(n, TILE["proj_n"])
    w_spec = pl.BlockSpec((None, d, tn), lambda j, m: (layer, 0, j),
                          pipeline_mode=pl.Buffered(1))
    o_spec = pl.BlockSpec((tm, tn), lambda j, m: (m, j))
    return pl.pallas_call(
        functools.partial(_multi_proj_kernel, posts=tuple(posts)),
        grid=(n // tn, t // tm),
        in_specs=[pl.BlockSpec((tm, d), lambda j, m: (m, 0))] + [w_spec] * len(weights),
        out_specs=[o_spec] * len(weights),
        out_shape=[jax.ShapeDtypeStruct((t, n), dt) for dt in out_dtypes],
        scratch_shapes=[pltpu.VMEM((d, tn), BF16)] * len(weights),
        compiler_params=_params(("arbitrary", "arbitrary"), 48),
        name=name,
    )(h, *weights)


def _proj_t_kernel(w_ref, h_ref, o_ref, w16_ref):
    _cast_weights(pl.program_id(1) == 0, [w_ref], [w16_ref])
    o_ref[...] = lax.dot_general(w16_ref[...], h_ref[...], _NT,
                                 preferred_element_type=F32).astype(o_ref.dtype)


def _proj_t(h, w_t, out_dtype, name):
    t, d = h.shape
    n = w_t.shape[0]
    tm, tn = _tile(t, TILE["proj_t_m"]), _tile(n, TILE["proj_t_n"])
    return pl.pallas_call(
        _proj_t_kernel,
        grid=(n // tn, t // tm),
        in_specs=[pl.BlockSpec((tn, d), lambda j, m: (j, 0), pipeline_mode=pl.Buffered(1)),
                  pl.BlockSpec((tm, d), lambda j, m: (m, 0))],
        out_specs=pl.BlockSpec((tn, tm), lambda j, m: (j, m)),
        out_shape=jax.ShapeDtypeStruct((n, t), out_dtype),
        scratch_shapes=[pltpu.VMEM((tn, d), BF16)],
        compiler_params=_params(("arbitrary", "arbitrary"), 32),
        name=name,
    )(w_t, h)


def _fox_cum_kernel(h_ref, wf_ref, bf_ref, pq_ref, pk_ref, cq_ref, ck_ref, qb_ref, kb_ref,
                    carry_ref):
    @pl.when(pl.program_id(1) == 0)
    def _():
        carry_ref[...] = jnp.zeros_like(carry_ref)

    log_f = _log_sigmoid(_dot(h_ref[...], wf_ref[...]) + bf_ref[...])
    ts = log_f.shape[0]
    tril = jnp.where(_tril_mask(ts), 1.0, 0.0).astype(BF16)
    cum = _cumsum_rows(tril, log_f) + carry_ref[...]
    carry_ref[...] = cum[ts - 1:ts, :]
    parts = jnp.concatenate(_split3(cum * LOG2E), axis=1)
    qb_ref[...] = (_dot(parts, pq_ref[...]) + cq_ref[...]).astype(qb_ref.dtype)
    kb_ref[...] = (_dot(parts, pk_ref[...]) + ck_ref[...]).astype(kb_ref.dtype)


BIAS_LANES = 8


def _bias_placement(heads):
    assert heads * BIAS_LANES <= LANES
    pq = np.zeros((3 * LANES, LANES), np.float32)
    pk = np.zeros((3 * LANES, LANES), np.float32)
    cq = np.zeros((1, LANES), np.float32)
    ck = np.zeros((1, LANES), np.float32)
    for hd in range(heads):
        base = hd * BIAS_LANES
        for term in range(3):
            pq[term * LANES + hd, base + term] = 1.0
            pk[term * LANES + hd, base + 3 + term] = -1.0
        cq[0, base + 3:base + 6] = 1.0
        ck[0, base:base + 3] = 1.0
    return jnp.asarray(pq, BF16), jnp.asarray(pk, BF16), jnp.asarray(cq), jnp.asarray(ck)


def _fox_cum(h, wf_pad, bf_pad, batch, seq, heads):
    t, d = h.shape
    ts = _tile(seq, TILE["rows"])
    ns = seq // ts
    width = LANES

    def const(b, s):
        return (0, 0)

    out_spec = pl.BlockSpec((ts, width), lambda b, s: (b * ns + s, 0))
    return pl.pallas_call(
        _fox_cum_kernel,
        grid=(batch, ns),
        in_specs=[
            pl.BlockSpec((ts, d), lambda b, s: (b * ns + s, 0)),
            pl.BlockSpec((d, LANES), const),
            pl.BlockSpec((1, LANES), const),
            pl.BlockSpec((3 * LANES, width), const),
            pl.BlockSpec((3 * LANES, width), const),
            pl.BlockSpec((1, width), const),
            pl.BlockSpec((1, width), const),
        ],
        out_specs=[out_spec, out_spec],
        out_shape=[jax.ShapeDtypeStruct((t, width), BF16)] * 2,
        scratch_shapes=[pltpu.VMEM((1, LANES), F32)],
        compiler_params=_params(("arbitrary", "arbitrary"), 32),
        name="fox_cum",
    )(h, wf_pad, bf_pad, *_bias_placement(heads))


def _fox_attn_kernel(q_ref, qb_ref, k_ref, kb_ref, vt_ref, g_ref, o_ref,
                     m_ref, acc_ref, l0_ref, l1_ref, s0_ref, s1_ref, smax0_ref, smax1_ref,
                     p0_ref, p1_ref, alpha0_ref, alpha1_ref, *, tq):
    nq = q_ref.shape[0] // tq
    pairs = [(qi, j) for qi in range(nq) for j in range(qi + 1)]
    s_slots, smax_slots = (s0_ref, s1_ref), (smax0_ref, smax1_ref)
    p_slots, alpha_slots = (p0_ref, p1_ref), (alpha0_ref, alpha1_ref)
    l_slots = (l0_ref, l1_ref)
    lane = lax.broadcasted_iota(jnp.int32, (tq, LANES), 1)
    own_lanes = lane // BIAS_LANES == pl.program_id(1)

    def rows(i):
        return slice(i * tq, (i + 1) * tq)

    def scores(t):
        qi, j = pairs[t]
        qb = jnp.where(own_lanes, qb_ref[rows(qi), :].astype(F32), 0.0).astype(BF16)
        q = jnp.concatenate([q_ref[rows(qi), :], qb], axis=1)
        k = jnp.concatenate([k_ref[rows(j), :], kb_ref[rows(j), :]], axis=1)
        s = lax.dot_general(k, q, _NT, preferred_element_type=F32)
        if j == qi:
            key = lax.broadcasted_iota(jnp.int32, (tq, tq), 0)
            query = lax.broadcasted_iota(jnp.int32, (tq, tq), 1)
            s = jnp.where(key <= query, s, -jnp.inf)
        s_slots[t % 2][...] = s
        smax_slots[t % 2][...] = jnp.max(s, axis=0, keepdims=True)

    def numerators(t):
        qi, j = pairs[t]
        l_ref = l_slots[qi % 2]
        if j == 0:
            m_new = smax_slots[t % 2][...]
            p = jnp.exp2(s_slots[t % 2][...] - m_new)
            l_ref[...] = jnp.sum(p, axis=0, keepdims=True)
        else:
            m_prev = m_ref[...]
            m_new = jnp.maximum(m_prev, smax_slots[t % 2][...])
            alpha = jnp.exp2(m_prev - m_new)
            alpha_slots[t % 2][...] = alpha
            p = jnp.exp2(s_slots[t % 2][...] - m_new)
            l_ref[...] = alpha * l_ref[...] + jnp.sum(p, axis=0, keepdims=True)
        p_slots[t % 2][...] = p.astype(BF16)
        m_ref[...] = m_new

    def values(t):
        qi, j = pairs[t]
        acc = _dot(vt_ref[:, rows(j)], p_slots[t % 2][...])
        if j > 0:
            acc = alpha_slots[t % 2][...] * acc_ref[...] + acc
        if j < qi:
            acc_ref[...] = acc
        else:
            o = (acc / l_slots[qi % 2][...]).T
            o_ref[rows(qi), :] = (o * g_ref[rows(qi), :]).astype(o_ref.dtype)

    for step in range(len(pairs) + 2):
        if step >= 2:
            values(step - 2)
        if step < len(pairs):
            scores(step)
        if 1 <= step <= len(pairs):
            numerators(step - 1)


def _fox_attn(q, qb, k, kb, vt, g, heads):
    batch, seq, d = q.shape
    dh = d // heads
    assert dh == LANES
    tq = _tile(seq, TILE["attn"])
    head_spec = pl.BlockSpec((None, seq, dh), lambda b, h: (b, 0, h))
    bias_spec = pl.BlockSpec((None, seq, LANES), lambda b, h: (b, 0, 0))
    stat = pltpu.VMEM((1, tq), F32)
    return pl.pallas_call(
        functools.partial(_fox_attn_kernel, tq=tq),
        grid=(batch, heads),
        in_specs=[
            head_spec, bias_spec, head_spec, bias_spec,
            pl.BlockSpec((dh, seq), lambda b, h: (h, b)),
            head_spec,
        ],
        out_specs=head_spec,
        out_shape=jax.ShapeDtypeStruct((batch, seq, d), BF16),
        scratch_shapes=[
            stat,
            pltpu.VMEM((dh, tq), F32),
            stat, stat,
            pltpu.VMEM((tq, tq), F32), pltpu.VMEM((tq, tq), F32),
            stat, stat,
            pltpu.VMEM((tq, tq), BF16), pltpu.VMEM((tq, tq), BF16),
            stat, stat,
        ],
        compiler_params=_params(("arbitrary", "arbitrary"), 40),
        name="fox_attn",
    )(q, qb, k, kb, vt, g)


def _gla_kernel(q_ref, k_ref, v_ref, a1_ref, wa2_ref, ba_ref, ng_ref, r_ref, o_ref,
                state_ref, *, n_sub):
    @pl.when(pl.program_id(2) == 0)
    def _():
        state_ref[...] = jnp.zeros_like(state_ref)

    c = GLA_CHUNK
    chunks = [slice(i * c, (i + 1) * c) for i in range(n_sub)]
    causal = _tril_mask(c)
    tril = jnp.where(causal, 1.0, 0.0).astype(BF16)
    ones = jnp.ones((c, LANES), BF16)

    gate = _dot(a1_ref[...].astype(BF16), wa2_ref[...]) + ba_ref[...]
    log_a = _log_sigmoid(gate) / GLA_GATE_TAU
    la_hi, la_mid, la_lo = _split3(log_a)
    cb = [(_dot(tril, la_lo[ch]) + _dot(tril, la_mid[ch])) + _dot(tril, la_hi[ch])
          for ch in chunks]
    decay_col = [((lax.dot_general(la_lo[ch], ones, _TN, preferred_element_type=F32)
                   + lax.dot_general(la_mid[ch], ones, _TN, preferred_element_type=F32))
                  + lax.dot_general(la_hi[ch], ones, _TN, preferred_element_type=F32))
                 for ch in chunks]
    cb_all = jnp.concatenate(cb, axis=0)
    cb_last = jnp.concatenate(
        [jnp.broadcast_to(x[c - 1:c, :], x.shape) for x in cb], axis=0)
    q = q_ref[...]
    k = k_ref[...]
    v = v_ref[...]
    q_dec = (q * jnp.exp(cb_all)).astype(BF16)
    k_inv = (k * jnp.exp(-cb_all)).astype(BF16)
    k_dec = (k * jnp.exp(cb_last - cb_all)).astype(BF16)
    att = [jnp.where(causal,
                     lax.dot_general(q_dec[ch], k_inv[ch], _NT, preferred_element_type=F32),
                     0.0).astype(BF16) for ch in chunks]
    o_intra = [_dot(a, v[ch]) for a, ch in zip(att, chunks)]
    kv = [lax.dot_general(k_dec[ch], v[ch], _TN, preferred_element_type=F32) for ch in chunks]

    state = state_ref[...]
    outs = []
    for i, ch in enumerate(chunks):
        outs.append(o_intra[i] + _dot(q_dec[ch], state.astype(BF16)))
        state = jnp.exp(decay_col[i][:, 0:1]) * state + kv[i]
    state_ref[...] = state

    o = jnp.concatenate(outs, axis=0)
    o = o * lax.rsqrt(jnp.mean(o * o, axis=-1, keepdims=True) + RMS_EPS) * ng_ref[...]
    o_ref[...] = (o * r_ref[...]).astype(o_ref.dtype)


def _gla(q, k, v, a1, wa2_pad, ba, norm_g, r, batch, seq):
    t, kdim = q.shape
    vdim = v.shape[1]
    dv = norm_g.shape[-1]
    heads = vdim // dv
    dk = kdim // heads
    rows = _tile(seq, TILE["gla_rows"])
    ns = seq // rows

    def row_spec(width):
        return pl.BlockSpec((rows, width), lambda b, h, s: (b * ns + s, h))

    return pl.pallas_call(
        functools.partial(_gla_kernel, n_sub=rows // GLA_CHUNK),
        grid=(batch, heads, ns),
        in_specs=[
            row_spec(dk), row_spec(dk), row_spec(dv),
            pl.BlockSpec((rows, LANES), lambda b, h, s: (b * ns + s, 0)),
            pl.BlockSpec((LANES, dk), lambda b, h, s: (0, h)),
            pl.BlockSpec((1, dk), lambda b, h, s: (0, h)),
            pl.BlockSpec((1, dv), lambda b, h, s: (0, 0)),
            row_spec(dv),
        ],
        out_specs=row_spec(dv),
        out_shape=jax.ShapeDtypeStruct((t, vdim), BF16),
        scratch_shapes=[pltpu.VMEM((dk, dv), F32)],
        compiler_params=_params(("arbitrary", "arbitrary", "arbitrary"), 48),
        name="gla_chunk",
    )(q, k, v, a1, wa2_pad, ba, norm_g, r)


def _conv_mix_kernel(h_ref, wb_ref, wc_ref, wu_ref, cw_ref, o_ref,
                     wb16_ref, wc16_ref, wu16_ref, carry_ref, *, tiles_per_seq):
    _cast_weights(pl.program_id(1) == 0, [wb_ref, wc_ref, wu_ref],
                  [wb16_ref, wc16_ref, wu16_ref])

    @pl.when(pl.program_id(1) % tiles_per_seq == 0)
    def _():
        carry_ref[...] = jnp.zeros_like(carry_ref)

    h = h_ref[...]
    cu = _dot(h, wc16_ref[...]) * _dot(h, wu16_ref[...])
    y = _causal_conv3(cu, carry_ref[...], cw_ref)
    carry_ref[...] = cu[cu.shape[0] - SUBLANES:, :]
    o_ref[...] = (_dot(h, wb16_ref[...]) * y).astype(o_ref.dtype)


def _conv_mix(h, w_in, conv_w, layer, seq):
    t, d = h.shape
    n = w_in.shape[2] // 3
    tm, tn = _tile(seq, TILE["proj_m"]), _tile(n, TILE["proj_n"])
    nj = n // tn

    def w_spec(part):
        return pl.BlockSpec((None, d, tn), lambda j, m: (layer, 0, part * nj + j),
                            pipeline_mode=pl.Buffered(1))

    return pl.pallas_call(
        functools.partial(_conv_mix_kernel, tiles_per_seq=seq // tm),
        grid=(nj, t // tm),
        in_specs=[
            pl.BlockSpec((tm, d), lambda j, m: (m, 0)),
            w_spec(0), w_spec(1), w_spec(2),
            pl.BlockSpec((None, conv_w.shape[1], tn), lambda j, m: (layer, 0, j)),
        ],
        out_specs=pl.BlockSpec((tm, tn), lambda j, m: (m, j)),
        out_shape=jax.ShapeDtypeStruct((t, n), BF16),
        scratch_shapes=[pltpu.VMEM((d, tn), BF16)] * 3 + [pltpu.VMEM((SUBLANES, tn), F32)],
        compiler_params=_params(("arbitrary", "arbitrary"), 48),
        name="conv_mix",
    )(h, w_in, w_in, w_in, conv_w)


def _ffn_up_kernel(h_ref, wa_ref, wu_ref, cwa_ref, cwu_ref, ba_ref, bu_ref, o_ref,
                   wa16_ref, wu16_ref, carry_a_ref, carry_u_ref, *, tiles_per_seq):
    _cast_weights(pl.program_id(1) == 0, [wa_ref, wu_ref], [wa16_ref, wu16_ref])

    @pl.when(pl.program_id(1) % tiles_per_seq == 0)
    def _():
        carry_a_ref[...] = jnp.zeros_like(carry_a_ref)
        carry_u_ref[...] = jnp.zeros_like(carry_u_ref)

    h = h_ref[...]
    za = _dot(h, wa16_ref[...])
    zu = _dot(h, wu16_ref[...])
    tm = za.shape[0]
    a = _causal_conv3(za, carry_a_ref[...], cwa_ref) + ba_ref[...]
    u = _causal_conv3(zu, carry_u_ref[...], cwu_ref) + bu_ref[...]
    carry_a_ref[...] = za[tm - SUBLANES:, :]
    carry_u_ref[...] = zu[tm - SUBLANES:, :]
    o_ref[...] = (_silu(a) * u).astype(o_ref.dtype)


def _ffn_up(h, w_up, conv_w, conv_b, layer, seq):
    t, d = h.shape
    f = w_up.shape[2] // 2
    tm, tn = _tile(seq, TILE["proj_m"]), _tile(f, TILE["proj_n"])
    nj = f // tn
    cw = conv_w.shape[1]
    bias = conv_b.reshape(conv_b.shape[0], 1, 2 * f)
    halo = pltpu.VMEM((SUBLANES, tn), F32)

    def spec(rows, half):
        return pl.BlockSpec((None, rows, tn), lambda j, m: (layer, 0, half * nj + j))

    return pl.pallas_call(
        functools.partial(_ffn_up_kernel, tiles_per_seq=seq // tm),
        grid=(nj, t // tm),
        in_specs=[
            pl.BlockSpec((tm, d), lambda j, m: (m, 0)),
            spec(d, 0), spec(d, 1), spec(cw, 0), spec(cw, 1), spec(1, 0), spec(1, 1),
        ],
        out_specs=pl.BlockSpec((tm, tn), lambda j, m: (m, j)),
        out_shape=jax.ShapeDtypeStruct((t, f), BF16),
        scratch_shapes=[pltpu.VMEM((d, tn), BF16)] * 2 + [halo] * 2,
        compiler_params=_params(("arbitrary", "arbitrary"), 56),
        name="ffn_up",
    )(h, w_up, w_up, conv_w, conv_w, bias, bias)


def _mm_ln_kernel(*refs, alpha, emit_h):
    a_ref, w_ref, x_ref, gate_ref, lng_ref, lnb_ref = refs[:6]
    if emit_h:
        sc_ref, sh_ref, xo_ref, h_ref = refs[6:10]
    else:
        xo_ref = refs[6]
    y_slots = refs[-2:]
    i = pl.program_id(0)
    last = pl.num_programs(0) - 1

    def project(slot):
        y_slots[slot][...] = _dot(a_ref[...], w_ref[...])

    def normalise(slot):
        z = alpha * x_ref[...] + gate_ref[0] * y_slots[slot][...]
        mu = jnp.mean(z, axis=-1, keepdims=True)
        zc = z - mu
        var = jnp.mean(zc * zc, axis=-1, keepdims=True)
        xn = zc * lax.rsqrt(var + LN_EPS) * lng_ref[...] + lnb_ref[...]
        xo_ref[...] = xn
        if emit_h:
            h_ref[...] = (xn * (1.0 + sc_ref[0]) + sh_ref[0]).astype(h_ref.dtype)

    @pl.when(i == 0)
    def _():
        project(0)

    for slot in (0, 1):
        mine = i % 2 == slot

        @pl.when(jnp.logical_and(mine, jnp.logical_and(i > 0, i < last)))
        def _():
            project(slot)
            normalise(1 - slot)

        @pl.when(jnp.logical_and(mine, i == last))
        def _():
            normalise(1 - slot)


def _mm_ln(a, w, x2, mod_tab, layer, gate_chunk, ln_g, ln_b, next_mod, alpha, seq, name):
    t, kdim = a.shape
    d = w.shape[1]
    tm = _tile(seq, TILE["ln_m"] if kdim <= TILE["ln_k_small"] else TILE["ln_m_big_k"])
    n = t // tm
    tiles_per_batch = seq // tm
    emit_h = next_mod is not None
    row = pl.BlockSpec((tm, d), lambda i: (jnp.maximum(i - 1, 0), 0))
    vec = pl.BlockSpec((1, d), lambda i: (0, 0))
    in_specs = [
        pl.BlockSpec((tm, kdim), lambda i: (jnp.minimum(i, n - 1), 0)),
        pl.BlockSpec((kdim, d), lambda i: (0, 0), pipeline_mode=pl.Buffered(1)),
        row,
        _mod_spec(d, layer, gate_chunk, tiles_per_batch, lag=1),
        vec, vec,
    ]
    args = [a, w, x2, mod_tab, ln_g.reshape(1, d), ln_b.reshape(1, d)]
    out_specs = [row]
    out_shape = [jax.ShapeDtypeStruct((t, d), F32)]
    if emit_h:
        nl, sc_chunk, sh_chunk = next_mod
        in_specs += [_mod_spec(d, nl, sc_chunk, tiles_per_batch, lag=1),
                     _mod_spec(d, nl, sh_chunk, tiles_per_batch, lag=1)]
        args += [mod_tab, mod_tab]
        out_specs.append(row)
        out_shape.append(jax.ShapeDtypeStruct((t, d), BF16))
    outs = pl.pallas_call(
        functools.partial(_mm_ln_kernel, alpha=alpha, emit_h=emit_h),
        grid=(n + 1,),
        in_specs=in_specs,
        out_specs=out_specs,
        out_shape=out_shape,
        scratch_shapes=[pltpu.VMEM((tm, d), F32), pltpu.VMEM((tm, d), F32)],
        compiler_params=_params(("arbitrary",), 56),
        name=name,
    )(*args)
    return (outs[0], outs[1]) if emit_h else (outs[0], None)


def _pad_cols(w, n):
    return jnp.pad(w, ((0, 0), (0, n - w.shape[1])))


def _fox_mixer(h, wq, wk, wv, wg, wf, bf, layer, batch, seq):
    t, d = h.shape
    heads = wf.shape[1]
    scale = (d // heads) ** -0.5 * LOG2E
    q, k, g = _multi_proj(
        h, [wq, wk, wg], layer,
        [lambda z: z * scale, lambda z: z, jax.nn.sigmoid],
        [BF16, BF16, F32], "fox_proj")
    vt = _proj_t(h, wv.T, BF16, "fox_proj_vt")
    qb, kb = _fox_cum(h, _pad_cols(wf, LANES).astype(BF16),
                      _pad_cols(bf.reshape(1, heads), LANES), batch, seq, heads)
    shape3 = (batch, seq, d)
    bias3 = (batch, seq, LANES)
    o = _fox_attn(q.reshape(shape3), qb.reshape(bias3), k.reshape(shape3), kb.reshape(bias3),
                  vt, g.reshape(shape3), heads)
    return o.reshape(t, d)


def _gla_mixer(h, wq, wk, wv, wa1, wa2, ba, wr, norm_g, layer, batch, seq):
    dv = norm_g.shape[-1]
    heads = wv.shape[2] // dv
    dk = wq.shape[2] // heads
    scale = dk ** -0.5
    q, k = _multi_proj(h, [wq, wk], layer,
                       [lambda z: z * scale, lambda z: z], [F32, F32], "gla_proj_qk")
    v, r = _multi_proj(h, [wv, wr], layer,
                       [lambda z: z, _silu], [BF16, F32], "gla_proj_vr")
    (a1,) = _multi_proj(h, [_pad_cols(wa1, LANES)[None]], 0, [lambda z: z], [F32],
                        "gla_proj_a1")
    wa2_pad = jnp.pad(wa2, ((0, LANES - wa2.shape[0]), (0, 0))).astype(BF16)
    return _gla(q, k, v, a1, wa2_pad, ba.reshape(1, -1), norm_g.reshape(1, dv), r, batch, seq)


def kernel(x, c, ada_w, ada_b, ln1_g, ln1_b, ln2_g, ln2_b, fox_wq, fox_wk, fox_wv, fox_wg, fox_wf, fox_bf, fox_wo, gla_wq, gla_wk, gla_wv, gla_wa1, gla_wa2, gla_ba, gla_wr, gla_norm_g, gla_wo, conv_w_in, conv_w, conv_w_out, ffn_w_up, ffn_conv_w, ffn_conv_b, ffn_w_down):
    batch, seq, d = x.shape
    depth = ada_w.shape[0]
    assert batch <= MOD_ROWS
    alpha = (2 * depth) ** 0.25
    t = batch * seq

    c_pad = jnp.pad(c, ((0, MOD_ROWS - batch), (0, 0)))
    mod_tab = _ada_mod(c_pad, ada_w, ada_b).reshape(depth * MOD_ROWS * 6, 1, d)

    x2 = x.reshape(t, d)
    h = _modulate(x2, mod_tab, 0, seq)
    for i in range(depth):
        kind, j = i % 3, i // 3
        if kind == 0:
            y = _fox_mixer(h, fox_wq, fox_wk, fox_wv[j], fox_wg, fox_wf[j], fox_bf[j],
                           j, batch, seq)
            wo = fox_wo[j]
        elif kind == 1:
            y = _gla_mixer(h, gla_wq, gla_wk, gla_wv, gla_wa1[j], gla_wa2[j], gla_ba[j],
                           gla_wr, gla_norm_g[j], j, batch, seq)
            wo = gla_wo[j]
        else:
            y = _conv_mix(h, conv_w_in, conv_w, j, seq)
            wo = conv_w_out[j]
        x2, h = _mm_ln(y, wo.astype(BF16), x2, mod_tab, i, 2, ln1_g[i], ln1_b[i], (i, 4, 3),
                       alpha, seq, "mixer_out_ln")
        act = _ffn_up(h, ffn_w_up, ffn_conv_w, ffn_conv_b, i, seq)
        next_mod = (i + 1, 1, 0) if i + 1 < depth else None
        x2, h = _mm_ln(act, ffn_w_down[i].astype(BF16), x2, mod_tab, i, 5, ln2_g[i], ln2_b[i],
                       next_mod, alpha, seq, "ffn_down_ln")
    return x2.reshape(batch, seq, d)
```

```python
import functools

import numpy as np
import jax
import jax.numpy as jnp
from jax import lax
from jax.experimental import pallas as pl
from jax.experimental.pallas import tpu as pltpu

F32 = jnp.float32
BF16 = jnp.bfloat16

LANES = 128
SUBLANES = 8
MOD_ROWS = SUBLANES
MIB = 1 << 20

LOG2E = 1.4426950408889634
GLA_CHUNK = 64
GLA_GATE_TAU = 16.0
LN_EPS = 1e-5
RMS_EPS = 1e-5
LAST_ROW_BLOCKS = 2

TILE = dict(
    ada_n=1024,
    rows=512,
    proj_m=1024,
    proj_n=512,
    ffn_m=1024,
    proj_t_m=512,
    proj_t_n=1024,
    attn=512,
    gla_rows=16 * GLA_CHUNK,
    ln_m=512,
    ln_m_big_k=256,
    ln_k_small=2048,
)

_NT = (((1,), (1,)), ((), ()))
_TN = (((0,), (0,)), ((), ()))


def _params(semantics, vmem_mib):
    return pltpu.CompilerParams(dimension_semantics=semantics,
                                vmem_limit_bytes=vmem_mib * MIB)


def _tile(n, pref):
    t = min(n, pref)
    assert n % t == 0, (n, pref)
    return t


def _dot(a, b):
    return jnp.dot(a, b, preferred_element_type=F32)


def _log_sigmoid(z):
    return jnp.minimum(z, 0.0) - jnp.log1p(jnp.exp(-jnp.abs(z)))


def _silu(z):
    return z * jax.nn.sigmoid(z)


def _split3(x):
    hi = x.astype(BF16)
    r1 = x - hi.astype(F32)
    mid = r1.astype(BF16)
    lo = (r1 - mid.astype(F32)).astype(BF16)
    return hi, mid, lo


def _tril_mask(n):
    r = lax.broadcasted_iota(jnp.int32, (n, n), 0)
    c = lax.broadcasted_iota(jnp.int32, (n, n), 1)
    return c <= r


def _cumsum_rows(tril_bf16, x):
    hi, mid, lo = _split3(x)
    return (_dot(tril_bf16, lo) + _dot(tril_bf16, mid)) + _dot(tril_bf16, hi)


def _causal_conv3(x, prev, w_ref):
    rows = lax.broadcasted_iota(jnp.int32, x.shape, 0)
    p6 = prev[SUBLANES - 2:SUBLANES - 1, :]
    p7 = prev[SUBLANES - 1:SUBLANES, :]
    x1 = jnp.where(rows == 0, p7, pltpu.roll(x, 1, 0))
    x2 = jnp.where(rows == 0, p6, jnp.where(rows == 1, p7, pltpu.roll(x, 2, 0)))
    return x2 * w_ref[0:1, :] + x1 * w_ref[1:2, :] + x * w_ref[2:3, :]


def _ada_kernel(c_ref, w_ref, b_ref, o_ref):
    cond = _silu(c_ref[...]).astype(BF16)
    o_ref[0] = _dot(cond, w_ref[0].astype(BF16)) + b_ref[0]


def _ada_mod(c_pad, ada_w, ada_b):
    depth, d, n = ada_w.shape
    tn = _tile(n, TILE["ada_n"])
    return pl.pallas_call(
        _ada_kernel,
        grid=(depth, n // tn),
        in_specs=[
            pl.BlockSpec((MOD_ROWS, d), lambda i, j: (0, 0)),
            pl.BlockSpec((1, d, tn), lambda i, j: (i, 0, j)),
            pl.BlockSpec((1, 1, tn), lambda i, j: (i, 0, j)),
        ],
        out_specs=pl.BlockSpec((1, MOD_ROWS, tn), lambda i, j: (i, 0, j)),
        out_shape=jax.ShapeDtypeStruct((depth, MOD_ROWS, n), F32),
        compiler_params=_params(("arbitrary", "arbitrary"), 40),
        name="ada_mod",
    )(c_pad, ada_w, ada_b.reshape(depth, 1, n))


def _mod_spec(d, layer, chunk, tiles_per_batch, lag=0):
    def index(m, *_):
        tile = jnp.maximum(m - lag, 0) if lag else m
        return ((layer * MOD_ROWS + tile // tiles_per_batch) * 6 + chunk, 0, 0)
    return pl.BlockSpec((1, 1, d), index)


def _modulate_kernel(x_ref, sc_ref, sh_ref, h_ref):
    h_ref[...] = (x_ref[...] * (1.0 + sc_ref[0]) + sh_ref[0]).astype(BF16)


def _modulate(x2, mod_tab, layer, seq):
    t, d = x2.shape
    tm = _tile(seq, TILE["rows"])
    return pl.pallas_call(
        _modulate_kernel,
        grid=(t // tm,),
        in_specs=[
            pl.BlockSpec((tm, d), lambda m: (m, 0)),
            _mod_spec(d, layer, 1, seq // tm),
            _mod_spec(d, layer, 0, seq // tm),
        ],
        out_specs=pl.BlockSpec((tm, d), lambda m: (m, 0)),
        out_shape=jax.ShapeDtypeStruct((t, d), BF16),
        compiler_params=_params(("arbitrary",), 32),
        name="modulate",
    )(x2, mod_tab, mod_tab)


def _cast_weights(first, w_refs, w16_refs):
    @pl.when(first)
    def _():
        for w_ref, w16_ref in zip(w_refs, w16_refs):
            w16_ref[...] = w_ref[...].astype(BF16)


def _multi_proj_kernel(h_ref, *refs, posts):
    n = len(posts)
    w_refs, o_refs, w16_refs = refs[:n], refs[n:2 * n], refs[2 * n:]
    _cast_weights(pl.program_id(1) == 0, w_refs, w16_refs)
    h = h_ref[...]
    for w16_ref, o_ref, post in zip(w16_refs[:-1], o_refs[:-1], posts[:-1]):
        o_ref[...] = post(_dot(h, w16_ref[...])).astype(o_ref.dtype)
    rb = h_ref.shape[0] // LAST_ROW_BLOCKS
    for i in range(LAST_ROW_BLOCKS):
        rows = slice(i * rb, (i + 1) * rb)
        o_refs[-1][rows, :] = posts[-1](_dot(h_ref[rows, :], w16_refs[-1][...])).astype(
            o_refs[-1].dtype)


def _multi_proj(h, weights, layer, posts, out_dtypes, name):
    t, d = h.shape
    n = weights[0].shape[2]
    tm, tn = _tile(t, TILE["proj_m"]), _tile(n, TILE["proj_n"])
    w_spec = pl.BlockSpec((None, d, tn), lambda j, m: (layer, 0, j),
                          pipeline_mode=pl.Buffered(1))
    o_spec = pl.BlockSpec((tm, tn), lambda j, m: (m, j))
    return pl.pallas_call(
        functools.partial(_multi_proj_kernel, posts=tuple(posts)),
        grid=(n // tn, t // tm),
        in_specs=[pl.BlockSpec((tm, d), lambda j, m: (m, 0))] + [w_spec] * len(weights),
        out_specs=[o_spec] * len(weights),
        out_shape=[jax.ShapeDtypeStruct((t, n), dt) for dt in out_dtypes],
        scratch_shapes=[pltpu.VMEM((d, tn), BF16)] * len(weights),
        compiler_params=_params(("arbitrary", "arbitrary"), 48),
        name=name,
    )(h, *weights)


def _proj_t_kernel(w_ref, h_ref, o_ref, w16_ref):
    _cast_weights(pl.program_id(1) == 0, [w_ref], [w16_ref])
    o_ref[...] = lax.dot_general(w16_ref[...], h_ref[...], _NT,
                                 preferred_element_type=F32).astype(o_ref.dtype)


def _proj_t(h, w_t, out_dtype, name):
    t, d = h.shape
    n = w_t.shape[0]
    tm, tn = _tile(t, TILE["proj_t_m"]), _tile(n, TILE["proj_t_n"])
    return pl.pallas_call(
        _proj_t_kernel,
        grid=(n // tn, t // tm),
        in_specs=[pl.BlockSpec((tn, d), lambda j, m: (j, 0), pipeline_mode=pl.Buffered(1)),
                  pl.BlockSpec((tm, d), lambda j, m: (m, 0))],
        out_specs=pl.BlockSpec((tn, tm), lambda j, m: (j, m)),
        out_shape=jax.ShapeDtypeStruct((n, t), out_dtype),
        scratch_shapes=[pltpu.VMEM((tn, d), BF16)],
        compiler_params=_params(("arbitrary", "arbitrary"), 32),
        name=name,
    )(w_t, h)


def _fox_cum_kernel(h_ref, wf_ref, bf_ref, pq_ref, pk_ref, cq_ref, ck_ref, qb_ref, kb_ref,
                    carry_ref):
    @pl.when(pl.program_id(1) == 0)
    def _():
        carry_ref[...] = jnp.zeros_like(carry_ref)

    log_f = _log_sigmoid(_dot(h_ref[...], wf_ref[...]) + bf_ref[...])
    ts = log_f.shape[0]
    tril = jnp.where(_tril_mask(ts), 1.0, 0.0).astype(BF16)
    cum = _cumsum_rows(tril, log_f) + carry_ref[...]
    carry_ref[...] = cum[ts - 1:ts, :]
    parts = jnp.concatenate(_split3(cum * LOG2E), axis=1)
    qb_ref[...] = (_dot(parts, pq_ref[...]) + cq_ref[...]).astype(qb_ref.dtype)
    kb_ref[...] = (_dot(parts, pk_ref[...]) + ck_ref[...]).astype(kb_ref.dtype)


BIAS_LANES = 8


def _bias_placement(heads):
    assert heads * BIAS_LANES <= LANES
    pq = np.zeros((3 * LANES, LANES), np.float32)
    pk = np.zeros((3 * LANES, LANES), np.float32)
    cq = np.zeros((1, LANES), np.float32)
    ck = np.zeros((1, LANES), np.float32)
    for hd in range(heads):
        base = hd * BIAS_LANES
        for term in range(3):
            pq[term * LANES + hd, base + term] = 1.0
            pk[term * LANES + hd, base + 3 + term] = -1.0
        cq[0, base + 3:base + 6] = 1.0
        ck[0, base:base + 3] = 1.0
    return jnp.asarray(pq, BF16), jnp.asarray(pk, BF16), jnp.asarray(cq), jnp.asarray(ck)


def _fox_cum(h, wf_pad, bf_pad, batch, seq, heads):
    t, d = h.shape
    ts = _tile(seq, TILE["rows"])
    ns = seq // ts
    width = LANES

    def const(b, s):
        return (0, 0)

    out_spec = pl.BlockSpec((ts, width), lambda b, s: (b * ns + s, 0))
    return pl.pallas_call(
        _fox_cum_kernel,
        grid=(batch, ns),
        in_specs=[
            pl.BlockSpec((ts, d), lambda b, s: (b * ns + s, 0)),
            pl.BlockSpec((d, LANES), const),
            pl.BlockSpec((1, LANES), const),
            pl.BlockSpec((3 * LANES, width), const),
            pl.BlockSpec((3 * LANES, width), const),
            pl.BlockSpec((1, width), const),
            pl.BlockSpec((1, width), const),
        ],
        out_specs=[out_spec, out_spec],
        out_shape=[jax.ShapeDtypeStruct((t, width), BF16)] * 2,
        scratch_shapes=[pltpu.VMEM((1, LANES), F32)],
        compiler_params=_params(("arbitrary", "arbitrary"), 32),
        name="fox_cum",
    )(h, wf_pad, bf_pad, *_bias_placement(heads))


def _fox_attn_kernel(q_ref, qb_ref, k_ref, kb_ref, vt_ref, g_ref, o_ref,
                     m_ref, acc_ref, l0_ref, l1_ref, s0_ref, s1_ref, smax0_ref, smax1_ref,
                     p0_ref, p1_ref, alpha0_ref, alpha1_ref, *, tq):
    nq = q_ref.shape[0] // tq
    pairs = [(qi, j) for qi in range(nq) for j in range(qi + 1)]
    s_slots, smax_slots = (s0_ref, s1_ref), (smax0_ref, smax1_ref)
    p_slots, alpha_slots = (p0_ref, p1_ref), (alpha0_ref, alpha1_ref)
    l_slots = (l0_ref, l1_ref)
    lane = lax.broadcasted_iota(jnp.int32, (tq, LANES), 1)
    own_lanes = lane // BIAS_LANES == pl.program_id(1)

    def rows(i):
        return slice(i * tq, (i + 1) * tq)

    def scores(t):
        qi, j = pairs[t]
        qb = jnp.where(own_lanes, qb_ref[rows(qi), :].astype(F32), 0.0).astype(BF16)
        q = jnp.concatenate([q_ref[rows(qi), :], qb], axis=1)
        k = jnp.concatenate([k_ref[rows(j), :], kb_ref[rows(j), :]], axis=1)
        s = lax.dot_general(k, q, _NT, preferred_element_type=F32)
        if j == qi:
            key = lax.broadcasted_iota(jnp.int32, (tq, tq), 0)
            query = lax.broadcasted_iota(jnp.int32, (tq, tq), 1)
            s = jnp.where(key <= query, s, -jnp.inf)
        s_slots[t % 2][...] = s
        smax_slots[t % 2][...] = jnp.max(s, axis=0, keepdims=True)

    def numerators(t):
        qi, j = pairs[t]
        l_ref = l_slots[qi % 2]
        if j == 0:
            m_new = smax_slots[t % 2][...]
            p = jnp.exp2(s_slots[t % 2][...] - m_new)
            l_ref[...] = jnp.sum(p, axis=0, keepdims=True)
        else:
            m_prev = m_ref[...]
            m_new = jnp.maximum(m_prev, smax_slots[t % 2][...])
            alpha = jnp.exp2(m_prev - m_new)
            alpha_slots[t % 2][...] = alpha
            p = jnp.exp2(s_slots[t % 2][...] - m_new)
            l_ref[...] = alpha * l_ref[...] + jnp.sum(p, axis=0, keepdims=True)
        p_slots[t % 2][...] = p.astype(BF16)
        m_ref[...] = m_new

    def values(t):
        qi, j = pairs[t]
        acc = _dot(vt_ref[:, rows(j)], p_slots[t % 2][...])
        if j > 0:
            acc = alpha_slots[t % 2][...] * acc_ref[...] + acc
        if j < qi:
            acc_ref[...] = acc
        else:
            o = (acc / l_slots[qi % 2][...]).T
            o_ref[rows(qi), :] = (o * g_ref[rows(qi), :]).astype(o_ref.dtype)

    for step in range(len(pairs) + 2):
        if step >= 2:
            values(step - 2)
        if step < len(pairs):
            scores(step)
        if 1 <= step <= len(pairs):
            numerators(step - 1)


def _fox_attn(q, qb, k, kb, vt, g, heads):
    batch, seq, d = q.shape
    dh = d // heads
    assert dh == LANES
    tq = _tile(seq, TILE["attn"])
    head_spec = pl.BlockSpec((None, seq, dh), lambda b, h: (b, 0, h))
    bias_spec = pl.BlockSpec((None, seq, LANES), lambda b, h: (b, 0, 0))
    stat = pltpu.VMEM((1, tq), F32)
    return pl.pallas_call(
        functools.partial(_fox_attn_kernel, tq=tq),
        grid=(batch, heads),
        in_specs=[
            head_spec, bias_spec, head_spec, bias_spec,
            pl.BlockSpec((dh, seq), lambda b, h: (h, b)),
            head_spec,
        ],
        out_specs=head_spec,
        out_shape=jax.ShapeDtypeStruct((batch, seq, d), BF16),
        scratch_shapes=[
            stat,
            pltpu.VMEM((dh, tq), F32),
            stat, stat,
            pltpu.VMEM((tq, tq), F32), pltpu.VMEM((tq, tq), F32),
            stat, stat,
            pltpu.VMEM((tq, tq), BF16), pltpu.VMEM((tq, tq), BF16),
            stat, stat,
        ],
        compiler_params=_params(("arbitrary", "arbitrary"), 40),
        name="fox_attn",
    )(q, qb, k, kb, vt, g)


def _gla_kernel(q_ref, k_ref, v_ref, a1_ref, wa2_ref, ba_ref, ng_ref, r_ref, o_ref,
                state_ref, *, n_sub):
    @pl.when(pl.program_id(2) == 0)
    def _():
        state_ref[...] = jnp.zeros_like(state_ref)

    c = GLA_CHUNK
    chunks = [slice(i * c, (i + 1) * c) for i in range(n_sub)]
    causal = _tril_mask(c)
    tril = jnp.where(causal, 1.0, 0.0).astype(BF16)
    ones = jnp.ones((c, LANES), BF16)

    gate = _dot(a1_ref[...].astype(BF16), wa2_ref[...]) + ba_ref[...]
    log_a = _log_sigmoid(gate) / GLA_GATE_TAU
    la_hi, la_mid, la_lo = _split3(log_a)
    cb = [(_dot(tril, la_lo[ch]) + _dot(tril, la_mid[ch])) + _dot(tril, la_hi[ch])
          for ch in chunks]
    decay_col = [((lax.dot_general(la_lo[ch], ones, _TN, preferred_element_type=F32)
                   + lax.dot_general(la_mid[ch], ones, _TN, preferred_element_type=F32))
                  + lax.dot_general(la_hi[ch], ones, _TN, preferred_element_type=F32))
                 for ch in chunks]
    cb_all = jnp.concatenate(cb, axis=0)
    cb_last = jnp.concatenate(
        [jnp.broadcast_to(x[c - 1:c, :], x.shape) for x in cb], axis=0)
    q = q_ref[...]
    k = k_ref[...]
    v = v_ref[...]
    q_dec = (q * jnp.exp(cb_all)).astype(BF16)
    k_inv = (k * jnp.exp(-cb_all)).astype(BF16)
    k_dec = (k * jnp.exp(cb_last - cb_all)).astype(BF16)
    att = [jnp.where(causal,
                     lax.dot_general(q_dec[ch], k_inv[ch], _NT, preferred_element_type=F32),
                     0.0).astype(BF16) for ch in chunks]
    o_intra = [_dot(a, v[ch]) for a, ch in zip(att, chunks)]
    kv = [lax.dot_general(k_dec[ch], v[ch], _TN, preferred_element_type=F32) for ch in chunks]

    state = state_ref[...]
    outs = []
    for i, ch in enumerate(chunks):
        outs.append(o_intra[i] + _dot(q_dec[ch], state.astype(BF16)))
        state = jnp.exp(decay_col[i][:, 0:1]) * state + kv[i]
    state_ref[...] = state

    o = jnp.concatenate(outs, axis=0)
    o = o * lax.rsqrt(jnp.mean(o * o, axis=-1, keepdims=True) + RMS_EPS) * ng_ref[...]
    o_ref[...] = (o * r_ref[...]).astype(o_ref.dtype)


def _gla(q, k, v, a1, wa2_pad, ba, norm_g, r, batch, seq):
    t, kdim = q.shape
    vdim = v.shape[1]
    dv = norm_g.shape[-1]
    heads = vdim // dv
    dk = kdim // heads
    rows = _tile(seq, TILE["gla_rows"])
    ns = seq // rows

    def row_spec(width):
        return pl.BlockSpec((rows, width), lambda b, h, s: (b * ns + s, h))

    return pl.pallas_call(
        functools.partial(_gla_kernel, n_sub=rows // GLA_CHUNK),
        grid=(batch, heads, ns),
        in_specs=[
            row_spec(dk), row_spec(dk), row_spec(dv),
            pl.BlockSpec((rows, LANES), lambda b, h, s: (b * ns + s, 0)),
            pl.BlockSpec((LANES, dk), lambda b, h, s: (0, h)),
            pl.BlockSpec((1, dk), lambda b, h, s: (0, h)),
            pl.BlockSpec((1, dv), lambda b, h, s: (0, 0)),
            row_spec(dv),
        ],
        out_specs=row_spec(dv),
        out_shape=jax.ShapeDtypeStruct((t, vdim), BF16),
        scratch_shapes=[pltpu.VMEM((dk, dv), F32)],
        compiler_params=_params(("arbitrary", "arbitrary", "arbitrary"), 48),
        name="gla_chunk",
    )(q, k, v, a1, wa2_pad, ba, norm_g, r)


def _conv_mix_kernel(h_ref, wb_ref, wc_ref, wu_ref, cw_ref, o_ref,
                     wb16_ref, wc16_ref, wu16_ref, carry_ref, *, tiles_per_seq):
    _cast_weights(pl.program_id(1) == 0, [wb_ref, wc_ref, wu_ref],
                  [wb16_ref, wc16_ref, wu16_ref])

    @pl.when(pl.program_id(1) % tiles_per_seq == 0)
    def _():
        carry_ref[...] = jnp.zeros_like(carry_ref)

    h = h_ref[...]
    cu = _dot(h, wc16_ref[...]) * _dot(h, wu16_ref[...])
    y = _causal_conv3(cu, carry_ref[...], cw_ref)
    carry_ref[...] = cu[cu.shape[0] - SUBLANES:, :]
    o_ref[...] = (_dot(h, wb16_ref[...]) * y).astype(o_ref.dtype)


def _conv_mix(h, w_in, conv_w, layer, seq):
    t, d = h.shape
    n = w_in.shape[2] // 3
    tm, tn = _tile(seq, TILE["proj_m"]), _tile(n, TILE["proj_n"])
    nj = n // tn

    def w_spec(part):
        return pl.BlockSpec((None, d, tn), lambda j, m: (layer, 0, part * nj + j),
                            pipeline_mode=pl.Buffered(1))

    return pl.pallas_call(
        functools.partial(_conv_mix_kernel, tiles_per_seq=seq // tm),
        grid=(nj, t // tm),
        in_specs=[
            pl.BlockSpec((tm, d), lambda j, m: (m, 0)),
            w_spec(0), w_spec(1), w_spec(2),
            pl.BlockSpec((None, conv_w.shape[1], tn), lambda j, m: (layer, 0, j)),
        ],
        out_specs=pl.BlockSpec((tm, tn), lambda j, m: (m, j)),
        out_shape=jax.ShapeDtypeStruct((t, n), BF16),
        scratch_shapes=[pltpu.VMEM((d, tn), BF16)] * 3 + [pltpu.VMEM((SUBLANES, tn), F32)],
        compiler_params=_params(("arbitrary", "arbitrary"), 48),
        name="conv_mix",
    )(h, w_in, w_in, w_in, conv_w)


def _ffn_up_kernel(h_ref, wa_ref, wu_ref, cwa_ref, cwu_ref, ba_ref, bu_ref, o_ref,
                   wa16_ref, wu16_ref, carry_a_ref, carry_u_ref, *, tiles_per_seq):
    _cast_weights(pl.program_id(1) == 0, [wa_ref, wu_ref], [wa16_ref, wu16_ref])

    @pl.when(pl.program_id(1) % tiles_per_seq == 0)
    def _():
        carry_a_ref[...] = jnp.zeros_like(carry_a_ref)
        carry_u_ref[...] = jnp.zeros_like(carry_u_ref)

    h = h_ref[...]
    za = _dot(h, wa16_ref[...])
    tm = za.shape[0]
    rb = tm // LAST_ROW_BLOCKS
    zu = jnp.concatenate([_dot(h_ref[i * rb:(i + 1) * rb, :], wu16_ref[...])
                          for i in range(LAST_ROW_BLOCKS)], axis=0)
    a = _causal_conv3(za, carry_a_ref[...], cwa_ref) + ba_ref[...]
    u = _causal_conv3(zu, carry_u_ref[...], cwu_ref) + bu_ref[...]
    carry_a_ref[...] = za[tm - SUBLANES:, :]
    carry_u_ref[...] = zu[tm - SUBLANES:, :]
    o_ref[...] = (_silu(a) * u).astype(o_ref.dtype)


def _ffn_up(h, w_up, conv_w, conv_b, layer, seq):
    t, d = h.shape
    f = w_up.shape[2] // 2
    tm, tn = _tile(seq, TILE["ffn_m"]), _tile(f, TILE["proj_n"])
    nj = f // tn
    cw = conv_w.shape[1]
    bias = conv_b.reshape(conv_b.shape[0], 1, 2 * f)
    halo = pltpu.VMEM((SUBLANES, tn), F32)

    def spec(rows, half):
        return pl.BlockSpec((None, rows, tn), lambda j, m: (layer, 0, half * nj + j))

    return pl.pallas_call(
        functools.partial(_ffn_up_kernel, tiles_per_seq=seq // tm),
        grid=(nj, t // tm),
        in_specs=[
            pl.BlockSpec((tm, d), lambda j, m: (m, 0)),
            spec(d, 0), spec(d, 1), spec(cw, 0), spec(cw, 1), spec(1, 0), spec(1, 1),
        ],
        out_specs=pl.BlockSpec((tm, tn), lambda j, m: (m, j)),
        out_shape=jax.ShapeDtypeStruct((t, f), BF16),
        scratch_shapes=[pltpu.VMEM((d, tn), BF16)] * 2 + [halo] * 2,
        compiler_params=_params(("arbitrary", "arbitrary"), 56),
        name="ffn_up",
    )(h, w_up, w_up, conv_w, conv_w, bias, bias)


def _mm_ln_kernel(*refs, alpha, emit_h):
    a_ref, w_ref, x_ref, gate_ref, lng_ref, lnb_ref = refs[:6]
    if emit_h:
        sc_ref, sh_ref, xo_ref, h_ref = refs[6:10]
    else:
        xo_ref = refs[6]
    y_slots = refs[-2:]
    i = pl.program_id(0)
    last = pl.num_programs(0) - 1

    def project(slot):
        y_slots[slot][...] = _dot(a_ref[...], w_ref[...])

    def normalise(slot):
        z = alpha * x_ref[...] + gate_ref[0] * y_slots[slot][...]
        mu = jnp.mean(z, axis=-1, keepdims=True)
        zc = z - mu
        var = jnp.mean(zc * zc, axis=-1, keepdims=True)
        xn = zc * lax.rsqrt(var + LN_EPS) * lng_ref[...] + lnb_ref[...]
        xo_ref[...] = xn
        if emit_h:
            h_ref[...] = (xn * (1.0 + sc_ref[0]) + sh_ref[0]).astype(h_ref.dtype)

    @pl.when(i == 0)
    def _():
        project(0)

    for slot in (0, 1):
        mine = i % 2 == slot

        @pl.when(jnp.logical_and(mine, jnp.logical_and(i > 0, i < last)))
        def _():
            project(slot)
            normalise(1 - slot)

        @pl.when(jnp.logical_and(mine, i == last))
        def _():
            normalise(1 - slot)


def _mm_ln(a, w, x2, mod_tab, layer, gate_chunk, ln_g, ln_b, next_mod, alpha, seq, name):
    t, kdim = a.shape
    d = w.shape[1]
    tm = _tile(seq, TILE["ln_m"] if kdim <= TILE["ln_k_small"] else TILE["ln_m_big_k"])
    n = t // tm
    tiles_per_batch = seq // tm
    emit_h = next_mod is not None
    row = pl.BlockSpec((tm, d), lambda i: (jnp.maximum(i - 1, 0), 0))
    vec = pl.BlockSpec((1, d), lambda i: (0, 0))
    in_specs = [
        pl.BlockSpec((tm, kdim), lambda i: (jnp.minimum(i, n - 1), 0)),
        pl.BlockSpec((kdim, d), lambda i: (0, 0), pipeline_mode=pl.Buffered(1)),
        row,
        _mod_spec(d, layer, gate_chunk, tiles_per_batch, lag=1),
        vec, vec,
    ]
    args = [a, w, x2, mod_tab, ln_g.reshape(1, d), ln_b.reshape(1, d)]
    out_specs = [row]
    out_shape = [jax.ShapeDtypeStruct((t, d), F32)]
    if emit_h:
        nl, sc_chunk, sh_chunk = next_mod
        in_specs += [_mod_spec(d, nl, sc_chunk, tiles_per_batch, lag=1),
                     _mod_spec(d, nl, sh_chunk, tiles_per_batch, lag=1)]
        args += [mod_tab, mod_tab]
        out_specs.append(row)
        out_shape.append(jax.ShapeDtypeStruct((t, d), BF16))
    outs = pl.pallas_call(
        functools.partial(_mm_ln_kernel, alpha=alpha, emit_h=emit_h),
        grid=(n + 1,),
        in_specs=in_specs,
        out_specs=out_specs,
        out_shape=out_shape,
        scratch_shapes=[pltpu.VMEM((tm, d), F32), pltpu.VMEM((tm, d), F32)],
        compiler_params=_params(("arbitrary",), 56),
        name=name,
    )(*args)
    return (outs[0], outs[1]) if emit_h else (outs[0], None)


def _pad_cols(w, n):
    return jnp.pad(w, ((0, 0), (0, n - w.shape[1])))


def _fox_mixer(h, wq, wk, wv, wg, wf, bf, layer, batch, seq):
    t, d = h.shape
    heads = wf.shape[1]
    scale = (d // heads) ** -0.5 * LOG2E
    q, k, g = _multi_proj(
        h, [wq, wk, wg], layer,
        [lambda z: z * scale, lambda z: z, jax.nn.sigmoid],
        [BF16, BF16, F32], "fox_proj")
    vt = _proj_t(h, wv.T, BF16, "fox_proj_vt")
    qb, kb = _fox_cum(h, _pad_cols(wf, LANES).astype(BF16),
                      _pad_cols(bf.reshape(1, heads), LANES), batch, seq, heads)
    shape3 = (batch, seq, d)
    bias3 = (batch, seq, LANES)
    o = _fox_attn(q.reshape(shape3), qb.reshape(bias3), k.reshape(shape3), kb.reshape(bias3),
                  vt, g.reshape(shape3), heads)
    return o.reshape(t, d)


def _gla_mixer(h, wq, wk, wv, wa1, wa2, ba, wr, norm_g, layer, batch, seq):
    dv = norm_g.shape[-1]
    heads = wv.shape[2] // dv
    dk = wq.shape[2] // heads
    scale = dk ** -0.5
    q, k = _multi_proj(h, [wq, wk], layer,
                       [lambda z: z * scale, lambda z: z], [F32, F32], "gla_proj_qk")
    v, r = _multi_proj(h, [wv, wr], layer,
                       [lambda z: z, _silu], [BF16, F32], "gla_proj_vr")
    (a1,) = _multi_proj(h, [_pad_cols(wa1, LANES)[None]], 0, [lambda z: z], [F32],
                        "gla_proj_a1")
    wa2_pad = jnp.pad(wa2, ((0, LANES - wa2.shape[0]), (0, 0))).astype(BF16)
    return _gla(q, k, v, a1, wa2_pad, ba.reshape(1, -1), norm_g.reshape(1, dv), r, batch, seq)


def kernel(x, c, ada_w, ada_b, ln1_g, ln1_b, ln2_g, ln2_b, fox_wq, fox_wk, fox_wv, fox_wg, fox_wf, fox_bf, fox_wo, gla_wq, gla_wk, gla_wv, gla_wa1, gla_wa2, gla_ba, gla_wr, gla_norm_g, gla_wo, conv_w_in, conv_w, conv_w_out, ffn_w_up, ffn_conv_w, ffn_conv_b, ffn_w_down):
    batch, seq, d = x.shape
    depth = ada_w.shape[0]
    assert batch <= MOD_ROWS
    alpha = (2 * depth) ** 0.25
    t = batch * seq

    c_pad = jnp.pad(c, ((0, MOD_ROWS - batch), (0, 0)))
    mod_tab = _ada_mod(c_pad, ada_w, ada_b).reshape(depth * MOD_ROWS * 6, 1, d)

    x2 = x.reshape(t, d)
    h = _modulate(x2, mod_tab, 0, seq)
    for i in range(depth):
        kind, j = i % 3, i // 3
        if kind == 0:
            y = _fox_mixer(h, fox_wq, fox_wk, fox_wv[j], fox_wg, fox_wf[j], fox_bf[j],
                           j, batch, seq)
            wo = fox_wo[j]
        elif kind == 1:
            y = _gla_mixer(h, gla_wq, gla_wk, gla_wv, gla_wa1[j], gla_wa2[j], gla_ba[j],
                           gla_wr, gla_norm_g[j], j, batch, seq)
            wo = gla_wo[j]
        else:
            y = _conv_mix(h, conv_w_in, conv_w, j, seq)
            wo = conv_w_out[j]
        x2, h = _mm_ln(y, wo.astype(BF16), x2, mod_tab, i, 2, ln1_g[i], ln1_b[i], (i, 4, 3),
                       alpha, seq, "mixer_out_ln")
        act = _ffn_up(h, ffn_w_up, ffn_conv_w, ffn_conv_b, i, seq)
        next_mod = (i + 1, 1, 0) if i + 1 < depth else None
        x2, h = _mm_ln(act, ffn_w_down[i].astype(BF16), x2, mod_tab, i, 5, ln2_g[i], ln2_b[i],
                       next_mod, alpha, seq, "ffn_down_ln")
    return x2.reshape(batch, seq, d)
```

```python
import functools

import numpy as np
import jax
import jax.numpy as jnp
from jax import lax
from jax.experimental import pallas as pl
from jax.experimental.pallas import tpu as pltpu

F32 = jnp.float32
BF16 = jnp.bfloat16

LANES = 128
SUBLANES = 8
MOD_ROWS = SUBLANES
MIB = 1 << 20

LOG2E = 1.4426950408889634
GLA_CHUNK = 64
GLA_GATE_TAU = 16.0
LN_EPS = 1e-5
RMS_EPS = 1e-5
LAST_BLOCK_SHARE = 2

TILE = dict(
    ada_n=1024,
    rows=512,
    proj_m=1024,
    proj_n=512,
    ffn_m=1024,
    proj_t_m=512,
    proj_t_n=1024,
    attn=512,
    gla_rows=16 * GLA_CHUNK,
    ln_m=512,
    ln_m_big_k=256,
    ln_k_small=2048,
)

_NT = (((1,), (1,)), ((), ()))
_TN = (((0,), (0,)), ((), ()))


def _params(semantics, vmem_mib):
    return pltpu.CompilerParams(dimension_semantics=semantics,
                                vmem_limit_bytes=vmem_mib * MIB)


def _tile(n, pref):
    t = min(n, pref)
    assert n % t == 0, (n, pref)
    return t


def _dot(a, b):
    return jnp.dot(a, b, preferred_element_type=F32)


def _last_row_blocks(tm):
    split = tm - tm // LAST_BLOCK_SHARE
    return slice(0, split), slice(split, tm)


def _log_sigmoid(z):
    return jnp.minimum(z, 0.0) - jnp.log1p(jnp.exp(-jnp.abs(z)))


def _silu(z):
    return z * jax.nn.sigmoid(z)


def _split3(x):
    hi = x.astype(BF16)
    r1 = x - hi.astype(F32)
    mid = r1.astype(BF16)
    lo = (r1 - mid.astype(F32)).astype(BF16)
    return hi, mid, lo


def _tril_mask(n):
    r = lax.broadcasted_iota(jnp.int32, (n, n), 0)
    c = lax.broadcasted_iota(jnp.int32, (n, n), 1)
    return c <= r


def _cumsum_rows(tril_bf16, x):
    hi, mid, lo = _split3(x)
    return (_dot(tril_bf16, lo) + _dot(tril_bf16, mid)) + _dot(tril_bf16, hi)


def _causal_conv3(x, prev, w_ref):
    rows = lax.broadcasted_iota(jnp.int32, x.shape, 0)
    p6 = prev[SUBLANES - 2:SUBLANES - 1, :]
    p7 = prev[SUBLANES - 1:SUBLANES, :]
    x1 = jnp.where(rows == 0, p7, pltpu.roll(x, 1, 0))
    x2 = jnp.where(rows == 0, p6, jnp.where(rows == 1, p7, pltpu.roll(x, 2, 0)))
    return x2 * w_ref[0:1, :] + x1 * w_ref[1:2, :] + x * w_ref[2:3, :]


def _ada_kernel(c_ref, w_ref, b_ref, o_ref):
    cond = _silu(c_ref[...]).astype(BF16)
    o_ref[0] = _dot(cond, w_ref[0].astype(BF16)) + b_ref[0]


def _ada_mod(c_pad, ada_w, ada_b):
    depth, d, n = ada_w.shape
    tn = _tile(n, TILE["ada_n"])
    return pl.pallas_call(
        _ada_kernel,
        grid=(depth, n // tn),
        in_specs=[
            pl.BlockSpec((MOD_ROWS, d), lambda i, j: (0, 0)),
            pl.BlockSpec((1, d, tn), lambda i, j: (i, 0, j)),
            pl.BlockSpec((1, 1, tn), lambda i, j: (i, 0, j)),
        ],
        out_specs=pl.BlockSpec((1, MOD_ROWS, tn), lambda i, j: (i, 0, j)),
        out_shape=jax.ShapeDtypeStruct((depth, MOD_ROWS, n), F32),
        compiler_params=_params(("arbitrary", "arbitrary"), 40),
        name="ada_mod",
    )(c_pad, ada_w, ada_b.reshape(depth, 1, n))


def _mod_spec(d, layer, chunk, tiles_per_batch, lag=0):
    def index(m, *_):
        tile = jnp.maximum(m - lag, 0) if lag else m
        return ((layer * MOD_ROWS + tile // tiles_per_batch) * 6 + chunk, 0, 0)
    return pl.BlockSpec((1, 1, d), index)


def _modulate_kernel(x_ref, sc_ref, sh_ref, h_ref):
    h_ref[...] = (x_ref[...] * (1.0 + sc_ref[0]) + sh_ref[0]).astype(BF16)


def _modulate(x2, mod_tab, layer, seq):
    t, d = x2.shape
    tm = _tile(seq, TILE["rows"])
    return pl.pallas_call(
        _modulate_kernel,
        grid=(t // tm,),
        in_specs=[
            pl.BlockSpec((tm, d), lambda m: (m, 0)),
            _mod_spec(d, layer, 1, seq // tm),
            _mod_spec(d, layer, 0, seq // tm),
        ],
        out_specs=pl.BlockSpec((tm, d), lambda m: (m, 0)),
        out_shape=jax.ShapeDtypeStruct((t, d), BF16),
        compiler_params=_params(("arbitrary",), 32),
        name="modulate",
    )(x2, mod_tab, mod_tab)


def _cast_weights(first, w_refs, w16_refs):
    @pl.when(first)
    def _():
        for w_ref, w16_ref in zip(w_refs, w16_refs):
            w16_ref[...] = w_ref[...].astype(BF16)


def _multi_proj_kernel(h_ref, *refs, posts):
    n = len(posts)
    w_refs, o_refs, w16_refs = refs[:n], refs[n:2 * n], refs[2 * n:]
    _cast_weights(pl.program_id(1) == 0, w_refs, w16_refs)
    h = h_ref[...]
    for w16_ref, o_ref, post in zip(w16_refs[:-1], o_refs[:-1], posts[:-1]):
        o_ref[...] = post(_dot(h, w16_ref[...])).astype(o_ref.dtype)
    for rows in _last_row_blocks(h_ref.shape[0]):
        o_refs[-1][rows, :] = posts[-1](_dot(h_ref[rows, :], w16_refs[-1][...])).astype(
            o_refs[-1].dtype)


def _multi_proj(h, weights, layer, posts, out_dtypes, name):
    t, d = h.shape
    n = weights[0].shape[2]
    tm, tn = _tile(t, TILE["proj_m"]), _tile(n, TILE["proj_n"])
    w_spec = pl.BlockSpec((None, d, tn), lambda j, m: (layer, 0, j),
                          pipeline_mode=pl.Buffered(1) if len(weights) > 2 else None)
    o_spec = pl.BlockSpec((tm, tn), lambda j, m: (m, j))
    return pl.pallas_call(
        functools.partial(_multi_proj_kernel, posts=tuple(posts)),
        grid=(n // tn, t // tm),
        in_specs=[pl.BlockSpec((tm, d), lambda j, m: (m, 0))] + [w_spec] * len(weights),
        out_specs=[o_spec] * len(weights),
        out_shape=[jax.ShapeDtypeStruct((t, n), dt) for dt in out_dtypes],
        scratch_shapes=[pltpu.VMEM((d, tn), BF16)] * len(weights),
        compiler_params=_params(("arbitrary", "arbitrary"), 48),
        name=name,
    )(h, *weights)


def _proj_t_kernel(w_ref, h_ref, o_ref, w16_ref):
    _cast_weights(pl.program_id(1) == 0, [w_ref], [w16_ref])
    o_ref[...] = lax.dot_general(w16_ref[...], h_ref[...], _NT,
                                 preferred_element_type=F32).astype(o_ref.dtype)


def _proj_t(h, w_t, out_dtype, name):
    t, d = h.shape
    n = w_t.shape[0]
    tm, tn = _tile(t, TILE["proj_t_m"]), _tile(n, TILE["proj_t_n"])
    return pl.pallas_call(
        _proj_t_kernel,
        grid=(n // tn, t // tm),
        in_specs=[pl.BlockSpec((tn, d), lambda j, m: (j, 0)),
                  pl.BlockSpec((tm, d), lambda j, m: (m, 0))],
        out_specs=pl.BlockSpec((tn, tm), lambda j, m: (j, m)),
        out_shape=jax.ShapeDtypeStruct((n, t), out_dtype),
        scratch_shapes=[pltpu.VMEM((tn, d), BF16)],
        compiler_params=_params(("arbitrary", "arbitrary"), 40),
        name=name,
    )(w_t, h)


def _fox_cum_kernel(h_ref, wf_ref, bf_ref, pq_ref, pk_ref, cq_ref, ck_ref, qb_ref, kb_ref,
                    carry_ref):
    @pl.when(pl.program_id(1) == 0)
    def _():
        carry_ref[...] = jnp.zeros_like(carry_ref)

    log_f = _log_sigmoid(_dot(h_ref[...], wf_ref[...]) + bf_ref[...])
    ts = log_f.shape[0]
    tril = jnp.where(_tril_mask(ts), 1.0, 0.0).astype(BF16)
    cum = _cumsum_rows(tril, log_f) + carry_ref[...]
    carry_ref[...] = cum[ts - 1:ts, :]
    parts = jnp.concatenate(_split3(cum * LOG2E), axis=1)
    qb_ref[...] = (_dot(parts, pq_ref[...]) + cq_ref[...]).astype(qb_ref.dtype)
    kb_ref[...] = (_dot(parts, pk_ref[...]) + ck_ref[...]).astype(kb_ref.dtype)


BIAS_LANES = 8


def _bias_placement(heads):
    assert heads * BIAS_LANES <= LANES
    pq = np.zeros((3 * LANES, LANES), np.float32)
    pk = np.zeros((3 * LANES, LANES), np.float32)
    cq = np.zeros((1, LANES), np.float32)
    ck = np.zeros((1, LANES), np.float32)
    for hd in range(heads):
        base = hd * BIAS_LANES
        for term in range(3):
            pq[term * LANES + hd, base + term] = 1.0
            pk[term * LANES + hd, base + 3 + term] = -1.0
        cq[0, base + 3:base + 6] = 1.0
        ck[0, base:base + 3] = 1.0
    return jnp.asarray(pq, BF16), jnp.asarray(pk, BF16), jnp.asarray(cq), jnp.asarray(ck)


def _fox_cum(h, wf_pad, bf_pad, batch, seq, heads):
    t, d = h.shape
    ts = _tile(seq, TILE["rows"])
    ns = seq // ts
    width = LANES

    def const(b, s):
        return (0, 0)

    out_spec = pl.BlockSpec((ts, width), lambda b, s: (b * ns + s, 0))
    return pl.pallas_call(
        _fox_cum_kernel,
        grid=(batch, ns),
        in_specs=[
            pl.BlockSpec((ts, d), lambda b, s: (b * ns + s, 0)),
            pl.BlockSpec((d, LANES), const),
            pl.BlockSpec((1, LANES), const),
            pl.BlockSpec((3 * LANES, width), const),
            pl.BlockSpec((3 * LANES, width), const),
            pl.BlockSpec((1, width), const),
            pl.BlockSpec((1, width), const),
        ],
        out_specs=[out_spec, out_spec],
        out_shape=[jax.ShapeDtypeStruct((t, width), BF16)] * 2,
        scratch_shapes=[pltpu.VMEM((1, LANES), F32)],
        compiler_params=_params(("arbitrary", "arbitrary"), 32),
        name="fox_cum",
    )(h, wf_pad, bf_pad, *_bias_placement(heads))


def _fox_attn_kernel(q_ref, qb_ref, k_ref, kb_ref, vt_ref, g_ref, o_ref,
                     m_ref, acc_ref, l0_ref, l1_ref, s0_ref, s1_ref, smax0_ref, smax1_ref,
                     p0_ref, p1_ref, alpha0_ref, alpha1_ref, *, tq):
    nq = q_ref.shape[0] // tq
    pairs = [(qi, j) for qi in range(nq) for j in range(qi + 1)]
    s_slots, smax_slots = (s0_ref, s1_ref), (smax0_ref, smax1_ref)
    p_slots, alpha_slots = (p0_ref, p1_ref), (alpha0_ref, alpha1_ref)
    l_slots = (l0_ref, l1_ref)
    lane = lax.broadcasted_iota(jnp.int32, (tq, LANES), 1)
    own_lanes = lane // BIAS_LANES == pl.program_id(1)

    def rows(i):
        return slice(i * tq, (i + 1) * tq)

    def scores(t):
        qi, j = pairs[t]
        qb = jnp.where(own_lanes, qb_ref[rows(qi), :].astype(F32), 0.0).astype(BF16)
        q = jnp.concatenate([q_ref[rows(qi), :], qb], axis=1)
        k = jnp.concatenate([k_ref[rows(j), :], kb_ref[rows(j), :]], axis=1)
        s = lax.dot_general(k, q, _NT, preferred_element_type=F32)
        if j == qi:
            key = lax.broadcasted_iota(jnp.int32, (tq, tq), 0)
            query = lax.broadcasted_iota(jnp.int32, (tq, tq), 1)
            s = jnp.where(key <= query, s, -jnp.inf)
        s_slots[t % 2][...] = s
        smax_slots[t % 2][...] = jnp.max(s, axis=0, keepdims=True)

    def numerators(t):
        qi, j = pairs[t]
        l_ref = l_slots[qi % 2]
        if j == 0:
            m_new = smax_slots[t % 2][...]
            p = jnp.exp2(s_slots[t % 2][...] - m_new)
            l_ref[...] = jnp.sum(p, axis=0, keepdims=True)
        else:
            m_prev = m_ref[...]
            m_new = jnp.maximum(m_prev, smax_slots[t % 2][...])
            alpha = jnp.exp2(m_prev - m_new)
            alpha_slots[t % 2][...] = alpha
            p = jnp.exp2(s_slots[t % 2][...] - m_new)
            l_ref[...] = alpha * l_ref[...] + jnp.sum(p, axis=0, keepdims=True)
        p_slots[t % 2][...] = p.astype(BF16)
        m_ref[...] = m_new

    def values(t):
        qi, j = pairs[t]
        acc = _dot(vt_ref[:, rows(j)], p_slots[t % 2][...])
        if j > 0:
            acc = alpha_slots[t % 2][...] * acc_ref[...] + acc
        if j < qi:
            acc_ref[...] = acc
        else:
            o = (acc / l_slots[qi % 2][...]).T
            o_ref[rows(qi), :] = (o * g_ref[rows(qi), :]).astype(o_ref.dtype)

    for step in range(len(pairs) + 2):
        if step >= 2:
            values(step - 2)
        if step < len(pairs):
            scores(step)
        if 1 <= step <= len(pairs):
            numerators(step - 1)


def _fox_attn(q, qb, k, kb, vt, g, heads):
    batch, seq, d = q.shape
    dh = d // heads
    assert dh == LANES
    tq = _tile(seq, TILE["attn"])
    head_spec = pl.BlockSpec((None, seq, dh), lambda b, h: (b, 0, h))
    bias_spec = pl.BlockSpec((None, seq, LANES), lambda b, h: (b, 0, 0))
    stat = pltpu.VMEM((1, tq), F32)
    return pl.pallas_call(
        functools.partial(_fox_attn_kernel, tq=tq),
        grid=(batch, heads),
        in_specs=[
            head_spec, bias_spec, head_spec, bias_spec,
            pl.BlockSpec((dh, seq), lambda b, h: (h, b)),
            head_spec,
        ],
        out_specs=head_spec,
        out_shape=jax.ShapeDtypeStruct((batch, seq, d), BF16),
        scratch_shapes=[
            stat,
            pltpu.VMEM((dh, tq), F32),
            stat, stat,
            pltpu.VMEM((tq, tq), F32), pltpu.VMEM((tq, tq), F32),
            stat, stat,
            pltpu.VMEM((tq, tq), BF16), pltpu.VMEM((tq, tq), BF16),
            stat, stat,
        ],
        compiler_params=_params(("arbitrary", "arbitrary"), 40),
        name="fox_attn",
    )(q, qb, k, kb, vt, g)


def _gla_kernel(q_ref, k_ref, v_ref, a1_ref, wa2_ref, ba_ref, ng_ref, r_ref, o_ref,
                state_ref, *, n_sub):
    @pl.when(pl.program_id(2) == 0)
    def _():
        state_ref[...] = jnp.zeros_like(state_ref)

    c = GLA_CHUNK
    chunks = [slice(i * c, (i + 1) * c) for i in range(n_sub)]
    causal = _tril_mask(c)
    tril = jnp.where(causal, 1.0, 0.0).astype(BF16)
    ones = jnp.ones((c, LANES), BF16)

    gate = _dot(a1_ref[...].astype(BF16), wa2_ref[...]) + ba_ref[...]
    log_a = _log_sigmoid(gate) / GLA_GATE_TAU
    la_hi, la_mid, la_lo = _split3(log_a)
    cb = [(_dot(tril, la_lo[ch]) + _dot(tril, la_mid[ch])) + _dot(tril, la_hi[ch])
          for ch in chunks]
    decay_col = [((lax.dot_general(la_lo[ch], ones, _TN, preferred_element_type=F32)
                   + lax.dot_general(la_mid[ch], ones, _TN, preferred_element_type=F32))
                  + lax.dot_general(la_hi[ch], ones, _TN, preferred_element_type=F32))
                 for ch in chunks]
    cb_all = jnp.concatenate(cb, axis=0)
    cb_last = jnp.concatenate(
        [jnp.broadcast_to(x[c - 1:c, :], x.shape) for x in cb], axis=0)
    q = q_ref[...]
    k = k_ref[...]
    v = v_ref[...]
    q_dec = (q * jnp.exp(cb_all)).astype(BF16)
    k_inv = (k * jnp.exp(-cb_all)).astype(BF16)
    k_dec = (k * jnp.exp(cb_last - cb_all)).astype(BF16)
    att = [jnp.where(causal,
                     lax.dot_general(q_dec[ch], k_inv[ch], _NT, preferred_element_type=F32),
                     0.0).astype(BF16) for ch in chunks]
    o_intra = [_dot(a, v[ch]) for a, ch in zip(att, chunks)]
    kv = [lax.dot_general(k_dec[ch], v[ch], _TN, preferred_element_type=F32) for ch in chunks]

    state = state_ref[...]
    outs = []
    for i, ch in enumerate(chunks):
        outs.append(o_intra[i] + _dot(q_dec[ch], state.astype(BF16)))
        state = jnp.exp(decay_col[i][:, 0:1]) * state + kv[i]
    state_ref[...] = state

    o = jnp.concatenate(outs, axis=0)
    o = o * lax.rsqrt(jnp.mean(o * o, axis=-1, keepdims=True) + RMS_EPS) * ng_ref[...]
    o_ref[...] = (o * r_ref[...]).astype(o_ref.dtype)


def _gla(q, k, v, a1, wa2_pad, ba, norm_g, r, batch, seq):
    t, kdim = q.shape
    vdim = v.shape[1]
    dv = norm_g.shape[-1]
    heads = vdim // dv
    dk = kdim // heads
    rows = _tile(seq, TILE["gla_rows"])
    ns = seq // rows

    def row_spec(width):
        return pl.BlockSpec((rows, width), lambda b, h, s: (b * ns + s, h))

    return pl.pallas_call(
        functools.partial(_gla_kernel, n_sub=rows // GLA_CHUNK),
        grid=(batch, heads, ns),
        in_specs=[
            row_spec(dk), row_spec(dk), row_spec(dv),
            pl.BlockSpec((rows, LANES), lambda b, h, s: (b * ns + s, 0)),
            pl.BlockSpec((LANES, dk), lambda b, h, s: (0, h)),
            pl.BlockSpec((1, dk), lambda b, h, s: (0, h)),
            pl.BlockSpec((1, dv), lambda b, h, s: (0, 0)),
            row_spec(dv),
        ],
        out_specs=row_spec(dv),
        out_shape=jax.ShapeDtypeStruct((t, vdim), BF16),
        scratch_shapes=[pltpu.VMEM((dk, dv), F32)],
        compiler_params=_params(("arbitrary", "arbitrary", "arbitrary"), 48),
        name="gla_chunk",
    )(q, k, v, a1, wa2_pad, ba, norm_g, r)


def _conv_mix_kernel(h_ref, wb_ref, wc_ref, wu_ref, cw_ref, o_ref,
                     wb16_ref, wc16_ref, wu16_ref, carry_ref, *, tiles_per_seq):
    _cast_weights(pl.program_id(1) == 0, [wb_ref, wc_ref, wu_ref],
                  [wb16_ref, wc16_ref, wu16_ref])

    @pl.when(pl.program_id(1) % tiles_per_seq == 0)
    def _():
        carry_ref[...] = jnp.zeros_like(carry_ref)

    h = h_ref[...]
    cu = _dot(h, wc16_ref[...]) * _dot(h, wu16_ref[...])
    y = _causal_conv3(cu, carry_ref[...], cw_ref)
    carry_ref[...] = cu[cu.shape[0] - SUBLANES:, :]
    o_ref[...] = (_dot(h, wb16_ref[...]) * y).astype(o_ref.dtype)


def _conv_mix(h, w_in, conv_w, layer, seq):
    t, d = h.shape
    n = w_in.shape[2] // 3
    tm, tn = _tile(seq, TILE["proj_m"]), _tile(n, TILE["proj_n"])
    nj = n // tn

    def w_spec(part):
        return pl.BlockSpec((None, d, tn), lambda j, m: (layer, 0, part * nj + j),
                            pipeline_mode=pl.Buffered(1))

    return pl.pallas_call(
        functools.partial(_conv_mix_kernel, tiles_per_seq=seq // tm),
        grid=(nj, t // tm),
        in_specs=[
            pl.BlockSpec((tm, d), lambda j, m: (m, 0)),
            w_spec(0), w_spec(1), w_spec(2),
            pl.BlockSpec((None, conv_w.shape[1], tn), lambda j, m: (layer, 0, j)),
        ],
        out_specs=pl.BlockSpec((tm, tn), lambda j, m: (m, j)),
        out_shape=jax.ShapeDtypeStruct((t, n), BF16),
        scratch_shapes=[pltpu.VMEM((d, tn), BF16)] * 3 + [pltpu.VMEM((SUBLANES, tn), F32)],
        compiler_params=_params(("arbitrary", "arbitrary"), 48),
        name="conv_mix",
    )(h, w_in, w_in, w_in, conv_w)


def _ffn_up_kernel(h_ref, wa_ref, wu_ref, cwa_ref, cwu_ref, ba_ref, bu_ref, o_ref,
                   wa16_ref, wu16_ref, carry_a_ref, carry_u_ref, *, tiles_per_seq):
    _cast_weights(pl.program_id(1) == 0, [wa_ref, wu_ref], [wa16_ref, wu16_ref])

    @pl.when(pl.program_id(1) % tiles_per_seq == 0)
    def _():
        carry_a_ref[...] = jnp.zeros_like(carry_a_ref)
        carry_u_ref[...] = jnp.zeros_like(carry_u_ref)

    h = h_ref[...]
    za = _dot(h, wa16_ref[...])
    tm = za.shape[0]
    zu = jnp.concatenate([_dot(h_ref[rows, :], wu16_ref[...])
                          for rows in _last_row_blocks(tm)], axis=0)
    a = _causal_conv3(za, carry_a_ref[...], cwa_ref) + ba_ref[...]
    u = _causal_conv3(zu, carry_u_ref[...], cwu_ref) + bu_ref[...]
    carry_a_ref[...] = za[tm - SUBLANES:, :]
    carry_u_ref[...] = zu[tm - SUBLANES:, :]
    o_ref[...] = (_silu(a) * u).astype(o_ref.dtype)


def _ffn_up(h, w_up, conv_w, conv_b, layer, seq):
    t, d = h.shape
    f = w_up.shape[2] // 2
    tm, tn = _tile(seq, TILE["ffn_m"]), _tile(f, TILE["proj_n"])
    nj = f // tn
    cw = conv_w.shape[1]
    bias = conv_b.reshape(conv_b.shape[0], 1, 2 * f)
    halo = pltpu.VMEM((SUBLANES, tn), F32)

    def spec(rows, half):
        return pl.BlockSpec((None, rows, tn), lambda j, m: (layer, 0, half * nj + j))

    return pl.pallas_call(
        functools.partial(_ffn_up_kernel, tiles_per_seq=seq // tm),
        grid=(nj, t // tm),
        in_specs=[
            pl.BlockSpec((tm, d), lambda j, m: (m, 0)),
            spec(d, 0), spec(d, 1), spec(cw, 0), spec(cw, 1), spec(1, 0), spec(1, 1),
        ],
        out_specs=pl.BlockSpec((tm, tn), lambda j, m: (m, j)),
        out_shape=jax.ShapeDtypeStruct((t, f), BF16),
        scratch_shapes=[pltpu.VMEM((d, tn), BF16)] * 2 + [halo] * 2,
        compiler_params=_params(("arbitrary", "arbitrary"), 56),
        name="ffn_up",
    )(h, w_up, w_up, conv_w, conv_w, bias, bias)


def _mm_ln_kernel(*refs, alpha, emit_h):
    a_ref, w_ref, x_ref, gate_ref, lng_ref, lnb_ref = refs[:6]
    if emit_h:
        sc_ref, sh_ref, xo_ref, h_ref = refs[6:10]
    else:
        xo_ref = refs[6]
    y_slots = refs[-2:]
    i = pl.program_id(0)
    last = pl.num_programs(0) - 1

    def project(slot):
        y_slots[slot][...] = _dot(a_ref[...], w_ref[...])

    def normalise(slot):
        z = alpha * x_ref[...] + gate_ref[0] * y_slots[slot][...]
        mu = jnp.mean(z, axis=-1, keepdims=True)
        zc = z - mu
        var = jnp.mean(zc * zc, axis=-1, keepdims=True)
        xn = zc * lax.rsqrt(var + LN_EPS) * lng_ref[...] + lnb_ref[...]
        xo_ref[...] = xn
        if emit_h:
            h_ref[...] = (xn * (1.0 + sc_ref[0]) + sh_ref[0]).astype(h_ref.dtype)

    @pl.when(i == 0)
    def _():
        project(0)

    for slot in (0, 1):
        mine = i % 2 == slot

        @pl.when(jnp.logical_and(mine, jnp.logical_and(i > 0, i < last)))
        def _():
            project(slot)
            normalise(1 - slot)

        @pl.when(jnp.logical_and(mine, i == last))
        def _():
            normalise(1 - slot)


def _mm_ln(a, w, w_layer, x2, mod_tab, layer, gate_chunk, ln_g, ln_b, next_mod, alpha, seq,
           name):
    t, kdim = a.shape
    d = w.shape[2]
    tm = _tile(seq, TILE["ln_m"] if kdim <= TILE["ln_k_small"] else TILE["ln_m_big_k"])
    n = t // tm
    tiles_per_batch = seq // tm
    emit_h = next_mod is not None
    row = pl.BlockSpec((tm, d), lambda i: (jnp.maximum(i - 1, 0), 0))
    vec = pl.BlockSpec((1, d), lambda i: (0, 0))
    in_specs = [
        pl.BlockSpec((tm, kdim), lambda i: (jnp.minimum(i, n - 1), 0)),
        pl.BlockSpec((None, kdim, d), lambda i: (w_layer, 0, 0), pipeline_mode=pl.Buffered(1)),
        row,
        _mod_spec(d, layer, gate_chunk, tiles_per_batch, lag=1),
        vec, vec,
    ]
    args = [a, w, x2, mod_tab, ln_g.reshape(1, d), ln_b.reshape(1, d)]
    out_specs = [row]
    out_shape = [jax.ShapeDtypeStruct((t, d), F32)]
    if emit_h:
        nl, sc_chunk, sh_chunk = next_mod
        in_specs += [_mod_spec(d, nl, sc_chunk, tiles_per_batch, lag=1),
                     _mod_spec(d, nl, sh_chunk, tiles_per_batch, lag=1)]
        args += [mod_tab, mod_tab]
        out_specs.append(row)
        out_shape.append(jax.ShapeDtypeStruct((t, d), BF16))
    outs = pl.pallas_call(
        functools.partial(_mm_ln_kernel, alpha=alpha, emit_h=emit_h),
        grid=(n + 1,),
        in_specs=in_specs,
        out_specs=out_specs,
        out_shape=out_shape,
        scratch_shapes=[pltpu.VMEM((tm, d), F32), pltpu.VMEM((tm, d), F32)],
        compiler_params=_params(("arbitrary",), 56),
        name=name,
    )(*args)
    return (outs[0], outs[1]) if emit_h else (outs[0], None)


def _pad_cols(w, n):
    return jnp.pad(w, ((0, 0), (0, n - w.shape[1])))


def _fox_mixer(h, wq, wk, wv, wg, wf, bf, layer, batch, seq):
    t, d = h.shape
    heads = wf.shape[1]
    scale = (d // heads) ** -0.5 * LOG2E
    q, k, g = _multi_proj(
        h, [wq, wk, wg], layer,
        [lambda z: z * scale, lambda z: z, jax.nn.sigmoid],
        [BF16, BF16, F32], "fox_proj")
    vt = _proj_t(h, wv.T, BF16, "fox_proj_vt")
    qb, kb = _fox_cum(h, _pad_cols(wf, LANES).astype(BF16),
                      _pad_cols(bf.reshape(1, heads), LANES), batch, seq, heads)
    shape3 = (batch, seq, d)
    bias3 = (batch, seq, LANES)
    o = _fox_attn(q.reshape(shape3), qb.reshape(bias3), k.reshape(shape3), kb.reshape(bias3),
                  vt, g.reshape(shape3), heads)
    return o.reshape(t, d)


def _gla_mixer(h, wq, wk, wv, wa1, wa2, ba, wr, norm_g, layer, batch, seq):
    dv = norm_g.shape[-1]
    heads = wv.shape[2] // dv
    dk = wq.shape[2] // heads
    scale = dk ** -0.5
    q, k = _multi_proj(h, [wq, wk], layer,
                       [lambda z: z * scale, lambda z: z], [F32, F32], "gla_proj_qk")
    v, r = _multi_proj(h, [wv, wr], layer,
                       [lambda z: z, _silu], [BF16, F32], "gla_proj_vr")
    (a1,) = _multi_proj(h, [_pad_cols(wa1, LANES)[None]], 0, [lambda z: z], [F32],
                        "gla_proj_a1")
    wa2_pad = jnp.pad(wa2, ((0, LANES - wa2.shape[0]), (0, 0))).astype(BF16)
    return _gla(q, k, v, a1, wa2_pad, ba.reshape(1, -1), norm_g.reshape(1, dv), r, batch, seq)


def kernel(x, c, ada_w, ada_b, ln1_g, ln1_b, ln2_g, ln2_b, fox_wq, fox_wk, fox_wv, fox_wg, fox_wf, fox_bf, fox_wo, gla_wq, gla_wk, gla_wv, gla_wa1, gla_wa2, gla_ba, gla_wr, gla_norm_g, gla_wo, conv_w_in, conv_w, conv_w_out, ffn_w_up, ffn_conv_w, ffn_conv_b, ffn_w_down):
    batch, seq, d = x.shape
    depth = ada_w.shape[0]
    assert batch <= MOD_ROWS
    alpha = (2 * depth) ** 0.25
    t = batch * seq

    c_pad = jnp.pad(c, ((0, MOD_ROWS - batch), (0, 0)))
    mod_tab = _ada_mod(c_pad, ada_w, ada_b).reshape(depth * MOD_ROWS * 6, 1, d)

    out_w = [fox_wo.astype(BF16), gla_wo.astype(BF16), conv_w_out.astype(BF16)]
    down_w = ffn_w_down.astype(BF16)

    x2 = x.reshape(t, d)
    h = _modulate(x2, mod_tab, 0, seq)
    for i in range(depth):
        kind, j = i % 3, i // 3
        if kind == 0:
            y = _fox_mixer(h, fox_wq, fox_wk, fox_wv[j], fox_wg, fox_wf[j], fox_bf[j],
                           j, batch, seq)
        elif kind == 1:
            y = _gla_mixer(h, gla_wq, gla_wk, gla_wv, gla_wa1[j], gla_wa2[j], gla_ba[j],
                           gla_wr, gla_norm_g[j], j, batch, seq)
        else:
            y = _conv_mix(h, conv_w_in, conv_w, j, seq)
        x2, h = _mm_ln(y, out_w[kind], j, x2, mod_tab, i, 2, ln1_g[i], ln1_b[i], (i, 4, 3),
                       alpha, seq, "mixer_out_ln")
        act = _ffn_up(h, ffn_w_up, ffn_conv_w, ffn_conv_b, i, seq)
        next_mod = (i + 1, 1, 0) if i + 1 < depth else None
        x2, h = _mm_ln(act, down_w, i, x2, mod_tab, i, 5, ln2_g[i], ln2_b[i],
                       next_mod, alpha, seq, "ffn_down_ln")
    return x2.reshape(batch, seq, d)
```

```python
import functools

import numpy as np
import jax
import jax.numpy as jnp
from jax import lax
from jax.experimental import pallas as pl
from jax.experimental.pallas import tpu as pltpu

F32 = jnp.float32
BF16 = jnp.bfloat16

LANES = 128
SUBLANES = 8
MOD_ROWS = SUBLANES
MIB = 1 << 20

LOG2E = 1.4426950408889634
GLA_CHUNK = 64
GLA_GATE_TAU = 16.0
LN_EPS = 1e-5
RMS_EPS = 1e-5
LAST_BLOCK_SHARE = 2

TILE = dict(
    ada_n=1024,
    rows=512,
    proj_m=1024,
    proj_n=512,
    ffn_m=1024,
    attn=512,
    gla_rows=16 * GLA_CHUNK,
    ln_m=512,
    ln_m_big_k=256,
    ln_k_small=2048,
)

_NT = (((1,), (1,)), ((), ()))
_TN = (((0,), (0,)), ((), ()))


def _params(semantics, vmem_mib):
    return pltpu.CompilerParams(dimension_semantics=semantics,
                                vmem_limit_bytes=vmem_mib * MIB)


def _tile(n, pref):
    t = min(n, pref)
    assert n % t == 0, (n, pref)
    return t


def _dot(a, b):
    return jnp.dot(a, b, preferred_element_type=F32)


def _last_row_blocks(tm):
    split = tm - tm // LAST_BLOCK_SHARE
    return slice(0, split), slice(split, tm)


def _log_sigmoid(z):
    return jnp.minimum(z, 0.0) - jnp.log1p(jnp.exp(-jnp.abs(z)))


def _silu(z):
    return z * jax.nn.sigmoid(z)


def _split3(x):
    hi = x.astype(BF16)
    r1 = x - hi.astype(F32)
    mid = r1.astype(BF16)
    lo = (r1 - mid.astype(F32)).astype(BF16)
    return hi, mid, lo


def _tril_mask(n):
    r = lax.broadcasted_iota(jnp.int32, (n, n), 0)
    c = lax.broadcasted_iota(jnp.int32, (n, n), 1)
    return c <= r


def _cumsum_rows(tril_bf16, x):
    hi, mid, lo = _split3(x)
    return (_dot(tril_bf16, lo) + _dot(tril_bf16, mid)) + _dot(tril_bf16, hi)


def _causal_conv3(x, prev, w_ref):
    rows = lax.broadcasted_iota(jnp.int32, x.shape, 0)
    p6 = prev[SUBLANES - 2:SUBLANES - 1, :]
    p7 = prev[SUBLANES - 1:SUBLANES, :]
    x1 = jnp.where(rows == 0, p7, pltpu.roll(x, 1, 0))
    x2 = jnp.where(rows == 0, p6, jnp.where(rows == 1, p7, pltpu.roll(x, 2, 0)))
    return x2 * w_ref[0:1, :] + x1 * w_ref[1:2, :] + x * w_ref[2:3, :]


def _ada_kernel(c_ref, w_ref, b_ref, o_ref):
    cond = _silu(c_ref[...]).astype(BF16)
    o_ref[0] = _dot(cond, w_ref[0].astype(BF16)) + b_ref[0]


def _ada_mod(c_pad, ada_w, ada_b):
    depth, d, n = ada_w.shape
    tn = _tile(n, TILE["ada_n"])
    return pl.pallas_call(
        _ada_kernel,
        grid=(depth, n // tn),
        in_specs=[
            pl.BlockSpec((MOD_ROWS, d), lambda i, j: (0, 0)),
            pl.BlockSpec((1, d, tn), lambda i, j: (i, 0, j)),
            pl.BlockSpec((1, 1, tn), lambda i, j: (i, 0, j)),
        ],
        out_specs=pl.BlockSpec((1, MOD_ROWS, tn), lambda i, j: (i, 0, j)),
        out_shape=jax.ShapeDtypeStruct((depth, MOD_ROWS, n), F32),
        compiler_params=_params(("arbitrary", "arbitrary"), 40),
        name="ada_mod",
    )(c_pad, ada_w, ada_b.reshape(depth, 1, n))


def _mod_spec(d, layer, chunk, tiles_per_batch, lag=0):
    def index(m, *_):
        tile = jnp.maximum(m - lag, 0) if lag else m
        return ((layer * MOD_ROWS + tile // tiles_per_batch) * 6 + chunk, 0, 0)
    return pl.BlockSpec((1, 1, d), index)


def _modulate_kernel(x_ref, sc_ref, sh_ref, h_ref):
    h_ref[...] = (x_ref[...] * (1.0 + sc_ref[0]) + sh_ref[0]).astype(BF16)


def _modulate(x2, mod_tab, layer, seq):
    t, d = x2.shape
    tm = _tile(seq, TILE["rows"])
    return pl.pallas_call(
        _modulate_kernel,
        grid=(t // tm,),
        in_specs=[
            pl.BlockSpec((tm, d), lambda m: (m, 0)),
            _mod_spec(d, layer, 1, seq // tm),
            _mod_spec(d, layer, 0, seq // tm),
        ],
        out_specs=pl.BlockSpec((tm, d), lambda m: (m, 0)),
        out_shape=jax.ShapeDtypeStruct((t, d), BF16),
        compiler_params=_params(("arbitrary",), 32),
        name="modulate",
    )(x2, mod_tab, mod_tab)


def _cast_weights(first, w_refs, w16_refs):
    @pl.when(first)
    def _():
        for w_ref, w16_ref in zip(w_refs, w16_refs):
            w16_ref[...] = w_ref[...].astype(BF16)


def _multi_proj_kernel(h_ref, *refs, posts, feature_major):
    n = len(posts)
    w_refs, o_refs, w16_refs = refs[:n], refs[n:2 * n], refs[2 * n:]
    _cast_weights(pl.program_id(1) == 0, w_refs, w16_refs)
    h = h_ref[...]
    for w16_ref, o_ref, post, fm in zip(w16_refs[:-1], o_refs[:-1], posts[:-1], feature_major):
        y = post(_dot(h, w16_ref[...]))
        o_ref[...] = (y.T if fm else y).astype(o_ref.dtype)
    for rows in _last_row_blocks(h_ref.shape[0]):
        o_refs[-1][rows, :] = posts[-1](_dot(h_ref[rows, :], w16_refs[-1][...])).astype(
            o_refs[-1].dtype)


def _multi_proj(h, weights, layer, posts, out_dtypes, name, feature_major=None):
    t, d = h.shape
    n = weights[0].shape[2]
    feature_major = tuple(feature_major or (False,) * len(weights))
    assert not feature_major[-1]
    tm, tn = _tile(t, TILE["proj_m"]), _tile(n, TILE["proj_n"])
    w_spec = pl.BlockSpec((None, d, tn), lambda j, m: (layer, 0, j),
                          pipeline_mode=pl.Buffered(1) if len(weights) > 2 else None)
    o_spec = pl.BlockSpec((tm, tn), lambda j, m: (m, j))
    o_spec_fm = pl.BlockSpec((tn, tm), lambda j, m: (j, m))
    return pl.pallas_call(
        functools.partial(_multi_proj_kernel, posts=tuple(posts), feature_major=feature_major),
        grid=(n // tn, t // tm),
        in_specs=[pl.BlockSpec((tm, d), lambda j, m: (m, 0))] + [w_spec] * len(weights),
        out_specs=[o_spec_fm if fm else o_spec for fm in feature_major],
        out_shape=[jax.ShapeDtypeStruct((n, t) if fm else (t, n), dt)
                   for fm, dt in zip(feature_major, out_dtypes)],
        scratch_shapes=[pltpu.VMEM((d, tn), BF16)] * len(weights),
        compiler_params=_params(("arbitrary", "arbitrary"), 48 if len(weights) < 4 else 56),
        name=name,
    )(h, *weights)


def _fox_cum_kernel(h_ref, wf_ref, bf_ref, pq_ref, pk_ref, cq_ref, ck_ref, qb_ref, kb_ref,
                    carry_ref):
    @pl.when(pl.program_id(1) == 0)
    def _():
        carry_ref[...] = jnp.zeros_like(carry_ref)

    log_f = _log_sigmoid(_dot(h_ref[...], wf_ref[...]) + bf_ref[...])
    ts = log_f.shape[0]
    tril = jnp.where(_tril_mask(ts), 1.0, 0.0).astype(BF16)
    cum = _cumsum_rows(tril, log_f) + carry_ref[...]
    carry_ref[...] = cum[ts - 1:ts, :]
    parts = jnp.concatenate(_split3(cum * LOG2E), axis=1)
    qb_ref[...] = (_dot(parts, pq_ref[...]) + cq_ref[...]).astype(qb_ref.dtype)
    kb_ref[...] = (_dot(parts, pk_ref[...]) + ck_ref[...]).astype(kb_ref.dtype)


BIAS_LANES = 8


def _bias_placement(heads):
    assert heads * BIAS_LANES <= LANES
    pq = np.zeros((3 * LANES, LANES), np.float32)
    pk = np.zeros((3 * LANES, LANES), np.float32)
    cq = np.zeros((1, LANES), np.float32)
    ck = np.zeros((1, LANES), np.float32)
    for hd in range(heads):
        base = hd * BIAS_LANES
        for term in range(3):
            pq[term * LANES + hd, base + term] = 1.0
            pk[term * LANES + hd, base + 3 + term] = -1.0
        cq[0, base + 3:base + 6] = 1.0
        ck[0, base:base + 3] = 1.0
    return jnp.asarray(pq, BF16), jnp.asarray(pk, BF16), jnp.asarray(cq), jnp.asarray(ck)


def _fox_cum(h, wf_pad, bf_pad, batch, seq, heads):
    t, d = h.shape
    ts = _tile(seq, TILE["rows"])
    ns = seq // ts
    width = LANES

    def const(b, s):
        return (0, 0)

    out_spec = pl.BlockSpec((ts, width), lambda b, s: (b * ns + s, 0))
    return pl.pallas_call(
        _fox_cum_kernel,
        grid=(batch, ns),
        in_specs=[
            pl.BlockSpec((ts, d), lambda b, s: (b * ns + s, 0)),
            pl.BlockSpec((d, LANES), const),
            pl.BlockSpec((1, LANES), const),
            pl.BlockSpec((3 * LANES, width), const),
            pl.BlockSpec((3 * LANES, width), const),
            pl.BlockSpec((1, width), const),
            pl.BlockSpec((1, width), const),
        ],
        out_specs=[out_spec, out_spec],
        out_shape=[jax.ShapeDtypeStruct((t, width), BF16)] * 2,
        scratch_shapes=[pltpu.VMEM((1, LANES), F32)],
        compiler_params=_params(("arbitrary", "arbitrary"), 32),
        name="fox_cum",
    )(h, wf_pad, bf_pad, *_bias_placement(heads))


def _fox_attn_kernel(q_ref, qb_ref, k_ref, kb_ref, vt_ref, g_ref, o_ref,
                     m_ref, acc_ref, l0_ref, l1_ref, s0_ref, s1_ref, smax0_ref, smax1_ref,
                     p0_ref, p1_ref, alpha0_ref, alpha1_ref, *, tq):
    nq = q_ref.shape[0] // tq
    pairs = [(qi, j) for qi in range(nq) for j in range(qi + 1)]
    s_slots, smax_slots = (s0_ref, s1_ref), (smax0_ref, smax1_ref)
    p_slots, alpha_slots = (p0_ref, p1_ref), (alpha0_ref, alpha1_ref)
    l_slots = (l0_ref, l1_ref)
    lane = lax.broadcasted_iota(jnp.int32, (tq, LANES), 1)
    own_lanes = lane // BIAS_LANES == pl.program_id(1)

    def rows(i):
        return slice(i * tq, (i + 1) * tq)

    def scores(t):
        qi, j = pairs[t]
        qb = jnp.where(own_lanes, qb_ref[rows(qi), :].astype(F32), 0.0).astype(BF16)
        q = jnp.concatenate([q_ref[rows(qi), :], qb], axis=1)
        k = jnp.concatenate([k_ref[rows(j), :], kb_ref[rows(j), :]], axis=1)
        s = lax.dot_general(k, q, _NT, preferred_element_type=F32)
        if j == qi:
            key = lax.broadcasted_iota(jnp.int32, (tq, tq), 0)
            query = lax.broadcasted_iota(jnp.int32, (tq, tq), 1)
            s = jnp.where(key <= query, s, -jnp.inf)
        s_slots[t % 2][...] = s
        smax_slots[t % 2][...] = jnp.max(s, axis=0, keepdims=True)

    def numerators(t):
        qi, j = pairs[t]
        l_ref = l_slots[qi % 2]
        if j == 0:
            m_new = smax_slots[t % 2][...]
            p = jnp.exp2(s_slots[t % 2][...] - m_new)
            l_ref[...] = jnp.sum(p, axis=0, keepdims=True)
        else:
            m_prev = m_ref[...]
            m_new = jnp.maximum(m_prev, smax_slots[t % 2][...])
            alpha = jnp.exp2(m_prev - m_new)
            alpha_slots[t % 2][...] = alpha
            p = jnp.exp2(s_slots[t % 2][...] - m_new)
            l_ref[...] = alpha * l_ref[...] + jnp.sum(p, axis=0, keepdims=True)
        p_slots[t % 2][...] = p.astype(BF16)
        m_ref[...] = m_new

    def values(t):
        qi, j = pairs[t]
        acc = _dot(vt_ref[:, rows(j)], p_slots[t % 2][...])
        if j > 0:
            acc = alpha_slots[t % 2][...] * acc_ref[...] + acc
        if j < qi:
            acc_ref[...] = acc
        else:
            o = (acc / l_slots[qi % 2][...]).T
            o_ref[rows(qi), :] = (o * g_ref[rows(qi), :]).astype(o_ref.dtype)

    for step in range(len(pairs) + 2):
        if step >= 2:
            values(step - 2)
        if step < len(pairs):
            scores(step)
        if 1 <= step <= len(pairs):
            numerators(step - 1)


def _fox_attn(q, qb, k, kb, vt, g, heads):
    batch, seq, d = q.shape
    dh = d // heads
    assert dh == LANES
    tq = _tile(seq, TILE["attn"])
    head_spec = pl.BlockSpec((None, seq, dh), lambda b, h: (b, 0, h))
    bias_spec = pl.BlockSpec((None, seq, LANES), lambda b, h: (b, 0, 0))
    stat = pltpu.VMEM((1, tq), F32)
    return pl.pallas_call(
        functools.partial(_fox_attn_kernel, tq=tq),
        grid=(batch, heads),
        in_specs=[
            head_spec, bias_spec, head_spec, bias_spec,
            pl.BlockSpec((dh, seq), lambda b, h: (h, b)),
            head_spec,
        ],
        out_specs=head_spec,
        out_shape=jax.ShapeDtypeStruct((batch, seq, d), BF16),
        scratch_shapes=[
            stat,
            pltpu.VMEM((dh, tq), F32),
            stat, stat,
            pltpu.VMEM((tq, tq), F32), pltpu.VMEM((tq, tq), F32),
            stat, stat,
            pltpu.VMEM((tq, tq), BF16), pltpu.VMEM((tq, tq), BF16),
            stat, stat,
        ],
        compiler_params=_params(("arbitrary", "arbitrary"), 40),
        name="fox_attn",
    )(q, qb, k, kb, vt, g)


def _gla_kernel(q_ref, k_ref, v_ref, a1_ref, wa2_ref, ba_ref, ng_ref, r_ref, o_ref,
                state_ref, *, n_sub):
    @pl.when(pl.program_id(2) == 0)
    def _():
        state_ref[...] = jnp.zeros_like(state_ref)

    c = GLA_CHUNK
    chunks = [slice(i * c, (i + 1) * c) for i in range(n_sub)]
    causal = _tril_mask(c)
    tril = jnp.where(causal, 1.0, 0.0).astype(BF16)
    ones = jnp.ones((c, LANES), BF16)

    gate = _dot(a1_ref[...].astype(BF16), wa2_ref[...]) + ba_ref[...]
    log_a = _log_sigmoid(gate) / GLA_GATE_TAU
    la_hi, la_mid, la_lo = _split3(log_a)
    cb = [(_dot(tril, la_lo[ch]) + _dot(tril, la_mid[ch])) + _dot(tril, la_hi[ch])
          for ch in chunks]
    decay_col = [((lax.dot_general(la_lo[ch], ones, _TN, preferred_element_type=F32)
                   + lax.dot_general(la_mid[ch], ones, _TN, preferred_element_type=F32))
                  + lax.dot_general(la_hi[ch], ones, _TN, preferred_element_type=F32))
                 for ch in chunks]
    cb_all = jnp.concatenate(cb, axis=0)
    cb_last = jnp.concatenate(
        [jnp.broadcast_to(x[c - 1:c, :], x.shape) for x in cb], axis=0)
    q = q_ref[...]
    k = k_ref[...]
    v = v_ref[...]
    q_dec = (q * jnp.exp(cb_all)).astype(BF16)
    k_inv = (k * jnp.exp(-cb_all)).astype(BF16)
    k_dec = (k * jnp.exp(cb_last - cb_all)).astype(BF16)
    att = [jnp.where(causal,
                     lax.dot_general(q_dec[ch], k_inv[ch], _NT, preferred_element_type=F32),
                     0.0).astype(BF16) for ch in chunks]
    o_intra = [_dot(a, v[ch]) for a, ch in zip(att, chunks)]
    kv = [lax.dot_general(k_dec[ch], v[ch], _TN, preferred_element_type=F32) for ch in chunks]

    state = state_ref[...]
    outs = []
    for i, ch in enumerate(chunks):
        outs.append(o_intra[i] + _dot(q_dec[ch], state.astype(BF16)))
        state = jnp.exp(decay_col[i][:, 0:1]) * state + kv[i]
    state_ref[...] = state

    o = jnp.concatenate(outs, axis=0)
    o = o * lax.rsqrt(jnp.mean(o * o, axis=-1, keepdims=True) + RMS_EPS) * ng_ref[...]
    o_ref[...] = (o * r_ref[...]).astype(o_ref.dtype)


def _gla(q, k, v, a1, wa2_pad, ba, norm_g, r, batch, seq):
    t, kdim = q.shape
    vdim = v.shape[1]
    dv = norm_g.shape[-1]
    heads = vdim // dv
    dk = kdim // heads
    rows = _tile(seq, TILE["gla_rows"])
    ns = seq // rows

    def row_spec(width):
        return pl.BlockSpec((rows, width), lambda b, h, s: (b * ns + s, h))

    return pl.pallas_call(
        functools.partial(_gla_kernel, n_sub=rows // GLA_CHUNK),
        grid=(batch, heads, ns),
        in_specs=[
            row_spec(dk), row_spec(dk), row_spec(dv),
            pl.BlockSpec((rows, LANES), lambda b, h, s: (b * ns + s, 0)),
            pl.BlockSpec((LANES, dk), lambda b, h, s: (0, h)),
            pl.BlockSpec((1, dk), lambda b, h, s: (0, h)),
            pl.BlockSpec((1, dv), lambda b, h, s: (0, 0)),
            row_spec(dv),
        ],
        out_specs=row_spec(dv),
        out_shape=jax.ShapeDtypeStruct((t, vdim), BF16),
        scratch_shapes=[pltpu.VMEM((dk, dv), F32)],
        compiler_params=_params(("arbitrary", "arbitrary", "arbitrary"), 48),
        name="gla_chunk",
    )(q, k, v, a1, wa2_pad, ba, norm_g, r)


def _conv_mix_kernel(h_ref, wb_ref, wc_ref, wu_ref, cw_ref, o_ref,
                     wb16_ref, wc16_ref, wu16_ref, carry_ref, *, tiles_per_seq):
    _cast_weights(pl.program_id(1) == 0, [wb_ref, wc_ref, wu_ref],
                  [wb16_ref, wc16_ref, wu16_ref])

    @pl.when(pl.program_id(1) % tiles_per_seq == 0)
    def _():
        carry_ref[...] = jnp.zeros_like(carry_ref)

    h = h_ref[...]
    cu = _dot(h, wc16_ref[...]) * _dot(h, wu16_ref[...])
    y = _causal_conv3(cu, carry_ref[...], cw_ref)
    carry_ref[...] = cu[cu.shape[0] - SUBLANES:, :]
    o_ref[...] = (_dot(h, wb16_ref[...]) * y).astype(o_ref.dtype)


def _conv_mix(h, w_in, conv_w, layer, seq):
    t, d = h.shape
    n = w_in.shape[2] // 3
    tm, tn = _tile(seq, TILE["proj_m"]), _tile(n, TILE["proj_n"])
    nj = n // tn

    def w_spec(part):
        return pl.BlockSpec((None, d, tn), lambda j, m: (layer, 0, part * nj + j),
                            pipeline_mode=pl.Buffered(1))

    return pl.pallas_call(
        functools.partial(_conv_mix_kernel, tiles_per_seq=seq // tm),
        grid=(nj, t // tm),
        in_specs=[
            pl.BlockSpec((tm, d), lambda j, m: (m, 0)),
            w_spec(0), w_spec(1), w_spec(2),
            pl.BlockSpec((None, conv_w.shape[1], tn), lambda j, m: (layer, 0, j)),
        ],
        out_specs=pl.BlockSpec((tm, tn), lambda j, m: (m, j)),
        out_shape=jax.ShapeDtypeStruct((t, n), BF16),
        scratch_shapes=[pltpu.VMEM((d, tn), BF16)] * 3 + [pltpu.VMEM((SUBLANES, tn), F32)],
        compiler_params=_params(("arbitrary", "arbitrary"), 48),
        name="conv_mix",
    )(h, w_in, w_in, w_in, conv_w)


def _ffn_up_kernel(h_ref, wa_ref, wu_ref, cwa_ref, cwu_ref, ba_ref, bu_ref, o_ref,
                   wa16_ref, wu16_ref, carry_a_ref, carry_u_ref, *, tiles_per_seq):
    _cast_weights(pl.program_id(1) == 0, [wa_ref, wu_ref], [wa16_ref, wu16_ref])

    @pl.when(pl.program_id(1) % tiles_per_seq == 0)
    def _():
        carry_a_ref[...] = jnp.zeros_like(carry_a_ref)
        carry_u_ref[...] = jnp.zeros_like(carry_u_ref)

    h = h_ref[...]
    za = _dot(h, wa16_ref[...])
    tm = za.shape[0]
    zu = jnp.concatenate([_dot(h_ref[rows, :], wu16_ref[...])
                          for rows in _last_row_blocks(tm)], axis=0)
    a = _causal_conv3(za, carry_a_ref[...], cwa_ref) + ba_ref[...]
    u = _causal_conv3(zu, carry_u_ref[...], cwu_ref) + bu_ref[...]
    carry_a_ref[...] = za[tm - SUBLANES:, :]
    carry_u_ref[...] = zu[tm - SUBLANES:, :]
    o_ref[...] = (_silu(a) * u).astype(o_ref.dtype)


def _ffn_up(h, w_up, conv_w, conv_b, layer, seq):
    t, d = h.shape
    f = w_up.shape[2] // 2
    tm, tn = _tile(seq, TILE["ffn_m"]), _tile(f, TILE["proj_n"])
    nj = f // tn
    cw = conv_w.shape[1]
    bias = conv_b.reshape(conv_b.shape[0], 1, 2 * f)
    halo = pltpu.VMEM((SUBLANES, tn), F32)

    def spec(rows, half):
        return pl.BlockSpec((None, rows, tn), lambda j, m: (layer, 0, half * nj + j))

    return pl.pallas_call(
        functools.partial(_ffn_up_kernel, tiles_per_seq=seq // tm),
        grid=(nj, t // tm),
        in_specs=[
            pl.BlockSpec((tm, d), lambda j, m: (m, 0)),
            spec(d, 0), spec(d, 1), spec(cw, 0), spec(cw, 1), spec(1, 0), spec(1, 1),
        ],
        out_specs=pl.BlockSpec((tm, tn), lambda j, m: (m, j)),
        out_shape=jax.ShapeDtypeStruct((t, f), BF16),
        scratch_shapes=[pltpu.VMEM((d, tn), BF16)] * 2 + [halo] * 2,
        compiler_params=_params(("arbitrary", "arbitrary"), 56),
        name="ffn_up",
    )(h, w_up, w_up, conv_w, conv_w, bias, bias)


def _mm_ln_kernel(*refs, alpha, emit_h):
    a_ref, w_ref, x_ref, gate_ref, lng_ref, lnb_ref = refs[:6]
    if emit_h:
        sc_ref, sh_ref, xo_ref, h_ref = refs[6:10]
    else:
        xo_ref = refs[6]
    y_slots = refs[-2:]
    i = pl.program_id(0)
    last = pl.num_programs(0) - 1

    def project(slot):
        y_slots[slot][...] = _dot(a_ref[...], w_ref[...])

    def normalise(slot):
        z = alpha * x_ref[...] + gate_ref[0] * y_slots[slot][...]
        mu = jnp.mean(z, axis=-1, keepdims=True)
        zc = z - mu
        var = jnp.mean(zc * zc, axis=-1, keepdims=True)
        xn = zc * lax.rsqrt(var + LN_EPS) * lng_ref[...] + lnb_ref[...]
        xo_ref[...] = xn
        if emit_h:
            h_ref[...] = (xn * (1.0 + sc_ref[0]) + sh_ref[0]).astype(h_ref.dtype)

    @pl.when(i == 0)
    def _():
        project(0)

    for slot in (0, 1):
        mine = i % 2 == slot

        @pl.when(jnp.logical_and(mine, jnp.logical_and(i > 0, i < last)))
        def _():
            project(slot)
            normalise(1 - slot)

        @pl.when(jnp.logical_and(mine, i == last))
        def _():
            normalise(1 - slot)


def _mm_ln(a, w, w_layer, x2, mod_tab, layer, gate_chunk, ln_g, ln_b, next_mod, alpha, seq,
           name):
    t, kdim = a.shape
    d = w.shape[2]
    tm = _tile(seq, TILE["ln_m"] if kdim <= TILE["ln_k_small"] else TILE["ln_m_big_k"])
    n = t // tm
    tiles_per_batch = seq // tm
    emit_h = next_mod is not None
    row = pl.BlockSpec((tm, d), lambda i: (jnp.maximum(i - 1, 0), 0))
    vec = pl.BlockSpec((1, d), lambda i: (0, 0))
    in_specs = [
        pl.BlockSpec((tm, kdim), lambda i: (jnp.minimum(i, n - 1), 0)),
        pl.BlockSpec((None, kdim, d), lambda i: (w_layer, 0, 0), pipeline_mode=pl.Buffered(1)),
        row,
        _mod_spec(d, layer, gate_chunk, tiles_per_batch, lag=1),
        vec, vec,
    ]
    args = [a, w, x2, mod_tab, ln_g.reshape(1, d), ln_b.reshape(1, d)]
    out_specs = [row]
    out_shape = [jax.ShapeDtypeStruct((t, d), F32)]
    if emit_h:
        nl, sc_chunk, sh_chunk = next_mod
        in_specs += [_mod_spec(d, nl, sc_chunk, tiles_per_batch, lag=1),
                     _mod_spec(d, nl, sh_chunk, tiles_per_batch, lag=1)]
        args += [mod_tab, mod_tab]
        out_specs.append(row)
        out_shape.append(jax.ShapeDtypeStruct((t, d), BF16))
    outs = pl.pallas_call(
        functools.partial(_mm_ln_kernel, alpha=alpha, emit_h=emit_h),
        grid=(n + 1,),
        in_specs=in_specs,
        out_specs=out_specs,
        out_shape=out_shape,
        scratch_shapes=[pltpu.VMEM((tm, d), F32), pltpu.VMEM((tm, d), F32)],
        compiler_params=_params(("arbitrary",), 56),
        name=name,
    )(*args)
    return (outs[0], outs[1]) if emit_h else (outs[0], None)


def _pad_cols(w, n):
    return jnp.pad(w, ((0, 0), (0, n - w.shape[1])))


def _fox_mixer(h, wq, wk, wv, wg, wf, bf, layer, batch, seq):
    t, d = h.shape
    heads = wf.shape[1]
    scale = (d // heads) ** -0.5 * LOG2E
    q, vt, g, k = _multi_proj(
        h, [wq, wv, wg, wk], layer,
        [lambda z: z * scale, lambda z: z, jax.nn.sigmoid, lambda z: z],
        [BF16, BF16, F32, BF16], "fox_proj", feature_major=(False, True, False, False))
    qb, kb = _fox_cum(h, _pad_cols(wf, LANES).astype(BF16),
                      _pad_cols(bf.reshape(1, heads), LANES), batch, seq, heads)
    shape3 = (batch, seq, d)
    bias3 = (batch, seq, LANES)
    o = _fox_attn(q.reshape(shape3), qb.reshape(bias3), k.reshape(shape3), kb.reshape(bias3),
                  vt, g.reshape(shape3), heads)
    return o.reshape(t, d)


def _gla_mixer(h, wq, wk, wv, wa1, wa2, ba, wr, norm_g, layer, batch, seq):
    dv = norm_g.shape[-1]
    heads = wv.shape[2] // dv
    dk = wq.shape[2] // heads
    scale = dk ** -0.5
    q, k = _multi_proj(h, [wq, wk], layer,
                       [lambda z: z * scale, lambda z: z], [F32, F32], "gla_proj_qk")
    v, r = _multi_proj(h, [wv, wr], layer,
                       [lambda z: z, _silu], [BF16, F32], "gla_proj_vr")
    (a1,) = _multi_proj(h, [_pad_cols(wa1, LANES)[None]], 0, [lambda z: z], [F32],
                        "gla_proj_a1")
    wa2_pad = jnp.pad(wa2, ((0, LANES - wa2.shape[0]), (0, 0))).astype(BF16)
    return _gla(q, k, v, a1, wa2_pad, ba.reshape(1, -1), norm_g.reshape(1, dv), r, batch, seq)


def kernel(x, c, ada_w, ada_b, ln1_g, ln1_b, ln2_g, ln2_b, fox_wq, fox_wk, fox_wv, fox_wg, fox_wf, fox_bf, fox_wo, gla_wq, gla_wk, gla_wv, gla_wa1, gla_wa2, gla_ba, gla_wr, gla_norm_g, gla_wo, conv_w_in, conv_w, conv_w_out, ffn_w_up, ffn_conv_w, ffn_conv_b, ffn_w_down):
    batch, seq, d = x.shape
    depth = ada_w.shape[0]
    assert batch <= MOD_ROWS
    alpha = (2 * depth) ** 0.25
    t = batch * seq

    c_pad = jnp.pad(c, ((0, MOD_ROWS - batch), (0, 0)))
    mod_tab = _ada_mod(c_pad, ada_w, ada_b).reshape(depth * MOD_ROWS * 6, 1, d)

    out_w = [fox_wo.astype(BF16), gla_wo.astype(BF16), conv_w_out.astype(BF16)]
    down_w = ffn_w_down.astype(BF16)

    x2 = x.reshape(t, d)
    h = _modulate(x2, mod_tab, 0, seq)
    for i in range(depth):
        kind, j = i % 3, i // 3
        if kind == 0:
            y = _fox_mixer(h, fox_wq, fox_wk, fox_wv, fox_wg, fox_wf[j], fox_bf[j],
                           j, batch, seq)
        elif kind == 1:
            y = _gla_mixer(h, gla_wq, gla_wk, gla_wv, gla_wa1[j], gla_wa2[j], gla_ba[j],
                           gla_wr, gla_norm_g[j], j, batch, seq)
        else:
            y = _conv_mix(h, conv_w_in, conv_w, j, seq)
        x2, h = _mm_ln(y, out_w[kind], j, x2, mod_tab, i, 2, ln1_g[i], ln1_b[i], (i, 4, 3),
                       alpha, seq, "mixer_out_ln")
        act = _ffn_up(h, ffn_w_up, ffn_conv_w, ffn_conv_b, i, seq)
        next_mod = (i + 1, 1, 0) if i + 1 < depth else None
        x2, h = _mm_ln(act, down_w, i, x2, mod_tab, i, 5, ln2_g[i], ln2_b[i],
                       next_mod, alpha, seq, "ffn_down_ln")
    return x2.reshape(batch, seq, d)
```
